```python
import jax, jax.numpy as jnp
from jax import lax
import numpy as np

D_MODEL = 1024
BATCH = 16
SEQ = 4096
DEPTH = 4
DEC_BATCH = 8
DEC_SEQ = 16
PAST_LEN = 1024

CHUNK = 64
D_MIX = D_MODEL
SSD_WIDTH = D_MIX // 2
SSD_HEAD_DIM = 64
SSD_HEADS = SSD_WIDTH // SSD_HEAD_DIM
SSD_GROUPS = 2
SSD_HPG = SSD_HEADS // SSD_GROUPS
D_STATE = 128
CONV_W = 4
CONV_DIM = SSD_WIDTH + 2 * SSD_GROUPS * D_STATE
SSD_CHUNK = CHUNK
CMLP_WIDTH = D_MIX - SSD_WIDTH
CMLP_GROUPS = 4
CMLP_GDIM = CMLP_WIDTH // CMLP_GROUPS
CMLP_CHUNK = 128
D_FF = 4 * D_MODEL
N_MOD = 6
IN_DIM = SSD_WIDTH + CONV_DIM + SSD_HEADS + 2 * CMLP_WIDTH
RMS_EPS = 1e-6
LN_EPS = 1e-5

kernel_name = 'hybrid_ssd_chunkmlp_stream_step'


def rms_norm(x, g):
    xf = x.astype(jnp.float32)
    y = xf * lax.rsqrt(jnp.mean(xf * xf, axis=-1, keepdims=True) + RMS_EPS)
    return (y * g.astype(jnp.float32)).astype(x.dtype)


def layer_norm(x, g, b):
    xf = x.astype(jnp.float32)
    mu = jnp.mean(xf, axis=-1, keepdims=True)
    xc = xf - mu
    y = xc * lax.rsqrt(jnp.mean(xc * xc, axis=-1, keepdims=True) + LN_EPS)
    return (y * g.astype(jnp.float32) + b.astype(jnp.float32)).astype(x.dtype)


def causal_conv(xbc, cache, w, b):
    L = xbc.shape[1]
    xp = jnp.concatenate([cache.astype(xbc.dtype), xbc], axis=1)
    out = b
    for k in range(CONV_W):
        out = out + w[k] * xp[:, k:k + L]
    return jax.nn.silu(out), xp[:, -(CONV_W - 1):]


def ssd_scan(x, dt, A, Bm, Cm, h0):
    b, L = x.shape[:2]
    Q = min(SSD_CHUNK, L)
    nc = L // Q
    x = x.reshape(b, nc, Q, SSD_GROUPS, SSD_HPG, SSD_HEAD_DIM)
    dt = dt.reshape(b, nc, Q, SSD_GROUPS, SSD_HPG)
    Bm = Bm.reshape(b, nc, Q, SSD_GROUPS, D_STATE)
    Cm = Cm.reshape(b, nc, Q, SSD_GROUPS, D_STATE)
    a_cum = jnp.cumsum(dt * A.reshape(SSD_GROUPS, SSD_HPG), axis=2)
    seg = a_cum[:, :, :, None] - a_cum[:, :, None, :]
    mask = jnp.tril(jnp.ones((Q, Q), dtype=bool))[:, :, None, None]
    decay = jnp.exp(jnp.where(mask, seg, -jnp.inf))
    cb = jnp.einsum('bcign,bcjgn->bcijg', Cm, Bm)
    xdt = x * dt[..., None]
    y_diag = jnp.einsum('bcijg,bcijgr,bcjgrp->bcigrp', cb, decay, xdt)
    a_last = a_cum[:, :, -1]
    w_end = jnp.exp(a_last[:, :, None] - a_cum) * dt
    states = jnp.einsum('bcjgn,bcjgr,bcjgrp->bcgrpn', Bm, w_end, x)

    def step(h, inp):
        s, al = inp
        return jnp.exp(al)[..., None, None] * h + s, h

    h0 = h0.reshape(b, SSD_GROUPS, SSD_HPG, SSD_HEAD_DIM, D_STATE)
    h_fin, h_in = lax.scan(step, h0, (jnp.moveaxis(states, 1, 0), jnp.moveaxis(a_last, 1, 0)))
    h_in = jnp.moveaxis(h_in, 0, 1)
    y_off = jnp.einsum('bcign,bcgrpn,bcigr->bcigrp', Cm, h_in, jnp.exp(a_cum))
    y = (y_diag + y_off).reshape(b, L, SSD_HEADS, SSD_HEAD_DIM)
    return y, h_fin.reshape(b, SSD_HEADS, SSD_HEAD_DIM, D_STATE)


def mixer(h, conv_cache, ssd_h0, w_in, conv_w, conv_b, dt_bias, a_log, d_skip,
          ssd_norm_g, v_ln_g, v_ln_b, w_s, b_s, w_out):
    b, L, _ = h.shape
    f32 = jnp.float32
    proj = h @ w_in
    z, xbc, dt_raw, uv = jnp.split(
        proj, [SSD_WIDTH, SSD_WIDTH + CONV_DIM, SSD_WIDTH + CONV_DIM + SSD_HEADS], axis=-1)
    xbc_act, conv_state = causal_conv(xbc, conv_cache, conv_w, conv_b)
    xs, Bm, Cm = jnp.split(xbc_act, [SSD_WIDTH, SSD_WIDTH + SSD_GROUPS * D_STATE], axis=-1)
    dt = jax.nn.softplus(dt_raw.astype(f32) + dt_bias.astype(f32))
    A = -jnp.exp(a_log.astype(f32))
    xh = xs.astype(f32).reshape(b, L, SSD_HEADS, SSD_HEAD_DIM)
    y, h_fin = ssd_scan(xh, dt, A,
                        Bm.astype(f32).reshape(b, L, SSD_GROUPS, D_STATE),
                        Cm.astype(f32).reshape(b, L, SSD_GROUPS, D_STATE),
                        ssd_h0.astype(f32))
    y = y + d_skip.astype(f32)[:, None] * xh
    y = y.reshape(b, L, SSD_WIDTH) * jax.nn.silu(z.astype(f32))
    y_ssd = rms_norm(y, ssd_norm_g).astype(h.dtype)
    u, v = jnp.split(jax.nn.gelu(uv), 2, axis=-1)
    v = layer_norm(v, v_ln_g, v_ln_b)
    Q = min(CMLP_CHUNK, L)
    nk = L // Q
    vq = v.reshape(b, nk, Q, CMLP_GROUPS, CMLP_GDIM)
    ws = jnp.tril(w_s[:, :Q, :Q])
    mixed = jnp.einsum('gij,bkjgd->bkigd', ws, vq) + b_s[:, :Q].T[None, None, :, :, None]
    y_cmlp = (u.reshape(b, nk, Q, CMLP_GROUPS, CMLP_GDIM) * mixed).reshape(b, L, CMLP_WIDTH)
    out = jnp.concatenate([y_ssd, y_cmlp.astype(h.dtype)], axis=-1) @ w_out
    return out, conv_state, h_fin, v


def block(x, c, conv_cache, ssd_h0, w_mod, b_mod, norm_g, w_in, conv_w, conv_b, dt_bias,
          a_log, d_skip, ssd_norm_g, v_ln_g, v_ln_b, w_s, b_s, w_out, w_ff1, w_ff2):
    mod = (jax.nn.silu(c) @ w_mod + b_mod)[:, None, :]
    sh1, sc1, g1, sh2, sc2, g2 = jnp.split(mod, N_MOD, axis=-1)
    h = rms_norm(x, norm_g[0]) * (1 + sc1) + sh1
    m, conv_state, h_fin, v = mixer(h, conv_cache, ssd_h0, w_in, conv_w, conv_b, dt_bias,
                                    a_log, d_skip, ssd_norm_g, v_ln_g, v_ln_b, w_s, b_s, w_out)
    x = x + g1 * rms_norm(m, norm_g[1])
    h = rms_norm(x, norm_g[2]) * (1 + sc2) + sh2
    f = jnp.square(jax.nn.relu(h @ w_ff1)) @ w_ff2
    x = x + g2 * rms_norm(f, norm_g[3])
    return x, conv_state, h_fin.astype(x.dtype), v


def setup_inputs(seed: int = 0) -> dict:
    key = jax.random.key(seed)
    ks = jax.random.split(key, 24)
    nrm = jax.random.normal
    dt0 = jnp.exp(jax.random.uniform(ks[10], (DEPTH, SSD_HEADS), minval=np.log(1e-3), maxval=np.log(1e-1)))
    return {
        'x_prompt': nrm(ks[0], (BATCH, SEQ, D_MODEL), jnp.float32),
        'x_sample': nrm(ks[1], (DEC_BATCH, DEC_SEQ, D_MODEL), jnp.float32),
        'state_conv': nrm(ks[2], (DEPTH, DEC_BATCH, CONV_W - 1, CONV_DIM), jnp.float32),
        'state_ssd': 0.1 * nrm(ks[3], (DEPTH, DEC_BATCH, SSD_HEADS, SSD_HEAD_DIM, D_STATE), jnp.float32),
        'c_prompt': nrm(ks[4], (BATCH, D_MODEL), jnp.float32),
        'c_sample': nrm(ks[5], (DEC_BATCH, D_MODEL), jnp.float32),
        'w_mod': 0.5 * D_MODEL ** -0.5 * nrm(ks[6], (DEPTH, D_MODEL, N_MOD * D_MODEL), jnp.float32),
        'b_mod': 0.01 * nrm(ks[7], (DEPTH, N_MOD * D_MODEL), jnp.float32),
        'norm_g': 1.0 + 0.02 * nrm(ks[8], (DEPTH, 4, D_MODEL), jnp.float32),
        'w_in': D_MODEL ** -0.5 * nrm(ks[9], (DEPTH, D_MODEL, IN_DIM), jnp.float32),
        'conv_w': CONV_W ** -0.5 * nrm(ks[11], (DEPTH, CONV_W, CONV_DIM), jnp.float32),
        'conv_b': 0.02 * nrm(ks[12], (DEPTH, CONV_DIM), jnp.float32),
        'dt_bias': dt0 + jnp.log(-jnp.expm1(-dt0)),
        'a_log': jnp.log(jax.random.uniform(ks[13], (DEPTH, SSD_HEADS), minval=1.0, maxval=16.0)),
        'd_skip': 1.0 + 0.1 * nrm(ks[14], (DEPTH, SSD_HEADS), jnp.float32),
        'ssd_norm_g': 1.0 + 0.02 * nrm(ks[15], (DEPTH, SSD_WIDTH), jnp.float32),
        'v_ln_g': 1.0 + 0.02 * nrm(ks[16], (DEPTH, CMLP_WIDTH), jnp.float32),
        'v_ln_b': 0.02 * nrm(ks[17], (DEPTH, CMLP_WIDTH), jnp.float32),
        'w_s': CMLP_CHUNK ** -0.5 * nrm(ks[18], (DEPTH, CMLP_GROUPS, CMLP_CHUNK, CMLP_CHUNK), jnp.float32),
        'b_s': 1.0 + 0.02 * nrm(ks[19], (DEPTH, CMLP_GROUPS, CMLP_CHUNK), jnp.float32),
        'w_out': D_MIX ** -0.5 * nrm(ks[20], (DEPTH, D_MIX, D_MODEL), jnp.float32),
        'w_ff1': D_MODEL ** -0.5 * nrm(ks[21], (DEPTH, D_MODEL, D_FF), jnp.float32),
        'w_ff2': D_FF ** -0.5 * nrm(ks[22], (DEPTH, D_FF, D_MODEL), jnp.float32),
    }


def reference(x_prompt, x_sample, state_conv, state_ssd, c_prompt, c_sample, w_mod, b_mod,
              norm_g, w_in, conv_w, conv_b, dt_bias, a_log, d_skip, ssd_norm_g, v_ln_g,
              v_ln_b, w_s, b_s, w_out, w_ff1, w_ff2):
    bp = x_prompt.shape[0]
    conv_p0 = jnp.zeros((bp, CONV_W - 1, CONV_DIM), x_prompt.dtype)
    ssd_p0 = jnp.zeros((bp, SSD_HEADS, SSD_HEAD_DIM, D_STATE), jnp.float32)
    xp, xs = x_prompt, x_sample
    conv_p, ssd_p, conv_s, ssd_s, v_s = [], [], [], [], []
    for l in range(DEPTH):
        lp = [w[l] for w in (w_mod, b_mod, norm_g, w_in, conv_w, conv_b, dt_bias, a_log, d_skip,
                             ssd_norm_g, v_ln_g, v_ln_b, w_s, b_s, w_out, w_ff1, w_ff2)]
        xp, cp, hp, _ = block(xp, c_prompt, conv_p0, ssd_p0, *lp)
        xs, cs, hs, vs = block(xs, c_sample, state_conv[l], state_ssd[l], *lp)
        conv_p.append(cp)
        ssd_p.append(hp)
        conv_s.append(cs)
        ssd_s.append(hs)
        v_s.append(vs)
    return (xp, xs, jnp.stack(conv_p), jnp.stack(ssd_p), jnp.stack(conv_s), jnp.stack(ssd_s), jnp.stack(v_s))
```

```python
import functools

import numpy as np
import jax
import jax.numpy as jnp
from jax import lax
from jax.experimental import pallas as pl
from jax.experimental.pallas import tpu as pltpu

F32 = jnp.float32
BF16 = jnp.bfloat16

D_MODEL = 1024
DEPTH = 4
SSD_WIDTH = 512
SSD_HEAD_DIM = 64
SSD_HEADS = 8
SSD_GROUPS = 2
SSD_HPG = 4
D_STATE = 128
CONV_W = 4
CONV_DIM = 1024
CMLP_WIDTH = 512
CMLP_GROUPS = 4
CMLP_GDIM = 128
CMLP_CHUNK = 128
D_FF = 4096
N_MOD = 6
RMS_EPS = 1e-6
LN_EPS = 1e-5

LANES = 128
CONV_HDR = 8
GROUP_W = SSD_HPG * SSD_HEAD_DIM
OFF_Z, OFF_XBC, OFF_U, OFF_V, OFF_DT = 0, 512, 1536, 2048, 2560
PROJ_W = OFF_DT + LANES
VMEM_LIMIT = 56 * 1024 * 1024


def _sigmoid(x):
    return 1.0 / (1.0 + jnp.exp(-x))


def _silu(x):
    return x * _sigmoid(x)


def _rms(x, eps=RMS_EPS):
    return x * lax.rsqrt(jnp.mean(x * x, axis=-1, keepdims=True) + eps)


def _dot(a, b):
    return jnp.dot(a, b, preferred_element_type=F32)


def _mod_kernel(c_ref, w_ref, b_ref, o_ref):
    s = _silu(c_ref[...]).astype(BF16)
    o_ref[0] = _dot(s, w_ref[0].astype(BF16)) + b_ref[0]


def _mod_call(c_all, w_mod, b_mod):
    nb = c_all.shape[0]
    tn = 1536
    n = N_MOD * D_MODEL
    return pl.pallas_call(
        _mod_kernel,
        grid=(DEPTH, n // tn),
        in_specs=[
            pl.BlockSpec((nb, D_MODEL), lambda l, j: (0, 0)),
            pl.BlockSpec((1, D_MODEL, tn), lambda l, j: (l, 0, j)),
            pl.BlockSpec((1, 1, tn), lambda l, j: (l, 0, j)),
        ],
        out_specs=pl.BlockSpec((1, nb, tn), lambda l, j: (l, 0, j)),
        out_shape=jax.ShapeDtypeStruct((DEPTH, nb, n), F32),
        compiler_params=pltpu.CompilerParams(
            dimension_semantics=("arbitrary", "arbitrary"), vmem_limit_bytes=VMEM_LIMIT),
        name="mod",
    )(c_all, w_mod, b_mod.reshape(DEPTH, 1, n))


def _mixer_kernel(x_ref, mod_ref, cinit_ref, sinit_ref, ng_ref, win_ref, cw_ref, cb_ref,
                  dtb_ref, alog_ref, dskip_ref, sng_ref, vg_ref, vb_ref, ws_ref, bs_ref,
                  e3_ref, wout_ref,
                  *refs, nbt, lt, ch, emit_v):
    if emit_v:
        x1_ref, cout_ref, sout_ref, vout_ref = refs[:4]
        scratch = refs[4:]
    else:
        x1_ref, cout_ref, sout_ref = refs[:3]
        vout_ref = None
        scratch = refs[3:]
    xbc_ref, z_ref, dt_ref, u_ref, v_ref, y_ref, s_ref = scratch

    t = pl.program_id(1)
    nt = pl.num_programs(1)
    tm = nbt * lt
    nc = lt // ch
    assert nbt == 1 or nc == 1

    def rows(m):
        if nbt == 1:
            return m
        return jnp.broadcast_to(m[:, None, :], (nbt, lt, m.shape[-1])).reshape(tm, m.shape[-1])

    x = x_ref[...].reshape(tm, D_MODEL)
    mod = mod_ref[...]
    sh1, sc1, g1 = mod[:, 0, :], mod[:, 1, :], mod[:, 2, :]
    h = _rms(x) * rows(ng_ref[0:1, :] * (1.0 + sc1)) + rows(sh1)
    hb = h.astype(BF16)

    for b in range(nbt):
        @pl.when(t == 0)
        def _():
            xbc_ref[b, 0:CONV_HDR, :] = cinit_ref[b]

        @pl.when(t > 0)
        def _():
            xbc_ref[b, 0:CONV_HDR, :] = xbc_ref[b, lt:lt + CONV_HDR, :]

    z_ref[...] = _dot(hb, win_ref[:, OFF_Z:OFF_XBC])
    xbc = _dot(hb, win_ref[:, OFF_XBC:OFF_U])
    for b in range(nbt):
        xbc_ref[b, CONV_HDR:CONV_HDR + lt, :] = xbc[b * lt:(b + 1) * lt, :]
    u_ref[...] = jax.nn.gelu(_dot(hb, win_ref[:, OFF_U:OFF_V]), approximate=True)
    v_ref[...] = jax.nn.gelu(_dot(hb, win_ref[:, OFF_V:OFF_DT]), approximate=True)
    dt_ref[...] = _dot(hb, win_ref[:, OFF_DT:PROJ_W])

    row_i = lax.broadcasted_iota(jnp.int32, (ch, LANES), 0)
    lane_i = lax.broadcasted_iota(jnp.int32, (ch, LANES), 1)
    ii = lax.broadcasted_iota(jnp.int32, (ch, ch), 0)
    jj = lax.broadcasted_iota(jnp.int32, (ch, ch), 1)
    causal = ii >= jj
    head_of_lane = lax.broadcasted_iota(jnp.int32, (ch, GROUP_W), 1) // SSD_HEAD_DIM

    def pack3(v):
        hi = v.astype(BF16).astype(F32)
        r1 = v - hi
        mid = r1.astype(BF16).astype(F32)
        lo = r1 - mid
        sel = jnp.where(lane_i < 8, hi, jnp.where(lane_i < 16, mid, jnp.where(lane_i < 24, lo, 0.0)))
        return sel.astype(BF16)

    def chunk(idx, carry):
        if nbt == 1:
            b, c = 0, idx
        else:
            b, c = idx, 0
        r0 = pl.multiple_of(c * ch, ch)
        row0 = pl.multiple_of(idx * ch, ch)
        first = jnp.logical_and(c == 0, t == 0)
        last = jnp.logical_and(c == nc - 1, t == nt - 1)

        @pl.when(first)
        def _():
            for g in range(SSD_GROUPS):
                s_ref[g] = sinit_ref[b, g * GROUP_W:(g + 1) * GROUP_W, :].T

        big = xbc_ref[b, pl.ds(r0, ch + CONV_HDR), :]
        acc = cb_ref[...]
        for k in range(CONV_W):
            o = CONV_HDR - (CONV_W - 1) + k
            acc = acc + cw_ref[k:k + 1, :] * big[o:o + ch, :]
        xbc_a = _silu(acc)
        xs = xbc_a[:, :SSD_WIDTH]
        bm = xbc_a[:, SSD_WIDTH:SSD_WIDTH + SSD_GROUPS * D_STATE]
        cm = xbc_a[:, SSD_WIDTH + SSD_GROUPS * D_STATE:]

        dtr = dt_ref[pl.ds(row0, ch), :] + dtb_ref[...]
        dt = jnp.maximum(dtr, 0.0) + jnp.log1p(jnp.exp(-jnp.abs(dtr)))
        a = dt * (-jnp.exp(alog_ref[...]))
        acum = a
        s = 1
        while s < ch:
            acum = acum + jnp.where(row_i >= s, pltpu.roll(acum, s, 0), 0.0)
            s *= 2
        ex = _dot(jnp.concatenate([pack3(dt), pack3(acum)], axis=1), e3_ref[...])
        dt_e = ex[:, :SSD_WIDTH]
        acum_e = ex[:, SSD_WIDTH:]
        ea_e = jnp.exp(acum_e)
        wend_e = jnp.exp(acum_e[ch - 1:ch, :] - acum_e) * dt_e
        xdt = (xs * dt_e).astype(BF16)
        xw = (xs * wend_e).astype(BF16)
        if ch == LANES:
            acum_t = acum.T
        else:
            acum_t = jnp.concatenate([acum, jnp.zeros((LANES - ch, LANES), F32)], axis=0).T[:, :ch]

        ys = []
        for g in range(SSD_GROUPS):
            cg = cm[:, g * D_STATE:(g + 1) * D_STATE].astype(BF16)
            bg = bm[:, g * D_STATE:(g + 1) * D_STATE].astype(BF16)
            cbm = lax.dot_general(cg, bg, (((1,), (1,)), ((), ())), preferred_element_type=F32)
            sg = s_ref[g]
            yg = _dot(cg, sg.astype(BF16)) * ea_e[:, g * GROUP_W:(g + 1) * GROUP_W]
            xdt_g = xdt[:, g * GROUP_W:(g + 1) * GROUP_W]
            for r in range(SSD_HPG):
                hh = g * SSD_HPG + r
                seg = acum[:, hh:hh + 1] - acum_t[hh:hh + 1, :]
                dec = jnp.where(causal, jnp.exp(seg), 0.0)
                mm = (cbm * dec).astype(BF16)
                rhs = jnp.where(head_of_lane == r, xdt_g, jnp.zeros_like(xdt_g))
                yg = yg + _dot(mm, rhs)
            ds = lax.dot_general(bg, xw[:, g * GROUP_W:(g + 1) * GROUP_W],
                                 (((0,), (0,)), ((), ())), preferred_element_type=F32)
            s_ref[g] = ea_e[ch - 1:ch, g * GROUP_W:(g + 1) * GROUP_W] * sg + ds
            ys.append(yg)
        y = jnp.concatenate(ys, axis=1) + dskip_ref[...] * xs
        y = y * _silu(z_ref[pl.ds(row0, ch), :])
        y = _rms(y) * sng_ref[...]
        y_ref[pl.ds(row0, ch), 0:SSD_WIDTH] = y.astype(BF16)

        @pl.when(last)
        def _():
            for g in range(SSD_GROUPS):
                sout_ref[b, g * GROUP_W:(g + 1) * GROUP_W, :] = s_ref[g].T

        u = u_ref[pl.ds(row0, ch), :]
        v = v_ref[pl.ds(row0, ch), :]
        vc = v - jnp.mean(v, axis=-1, keepdims=True)
        vn = vc * lax.rsqrt(jnp.mean(vc * vc, axis=-1, keepdims=True) + LN_EPS) * vg_ref[...] + vb_ref[...]
        if emit_v:
            vout_ref[b, pl.ds(r0, ch), :] = vn
        vnb = vn.astype(BF16)
        for g in range(CMLP_GROUPS):
            sl = slice(g * CMLP_GDIM, (g + 1) * CMLP_GDIM)
            wg = jnp.where(causal, ws_ref[g], 0.0).astype(BF16)
            mixed = _dot(wg, vnb[:, sl]) + bs_ref[:, sl]
            y_ref[pl.ds(row0, ch), SSD_WIDTH + g * CMLP_GDIM:SSD_WIDTH + (g + 1) * CMLP_GDIM] = (
                u[:, sl] * mixed).astype(BF16)
        return carry

    lax.fori_loop(0, nbt * nc, chunk, 0)

    @pl.when(t == nt - 1)
    def _():
        for b in range(nbt):
            cout_ref[b] = xbc_ref[b, lt:lt + CONV_HDR, :]

    m = _dot(y_ref[...], wout_ref[...])
    x1 = x + rows(g1) * (_rms(m) * ng_ref[1:2, :])
    x1_ref[...] = x1.reshape(nbt, lt, D_MODEL)


def _mixer_call(x, mod, cinit, sinit, ng, win, cw, cb, dtb, alog, dskip, sng, vg, vb, ws, bs,
                e3, wout, *, nbt, lt, ch, emit_v):
    nb, seq, _ = x.shape
    tm = nbt * lt
    grid = (nb // nbt, seq // lt)

    def const(shape):
        return pl.BlockSpec(shape, lambda bi, ti: (0,) * len(shape))

    in_specs = [
        pl.BlockSpec((nbt, lt, D_MODEL), lambda bi, ti: (bi, ti, 0)),
        pl.BlockSpec((nbt, N_MOD, D_MODEL), lambda bi, ti: (bi, 0, 0)),
        pl.BlockSpec((nbt, CONV_HDR, CONV_DIM), lambda bi, ti: (bi, 0, 0)),
        pl.BlockSpec((nbt, SSD_WIDTH, D_STATE), lambda bi, ti: (bi, 0, 0)),
        const((4, D_MODEL)),
        const((D_MODEL, PROJ_W)),
        const((CONV_W, CONV_DIM)),
        const((1, CONV_DIM)),
        const((1, LANES)),
        const((1, LANES)),
        const((1, SSD_WIDTH)),
        const((1, SSD_WIDTH)),
        const((1, CMLP_WIDTH)),
        const((1, CMLP_WIDTH)),
        const((CMLP_GROUPS, ch, ch)),
        const((ch, CMLP_WIDTH)),
        const((2 * LANES, 2 * SSD_WIDTH)),
        const((D_MODEL, D_MODEL)),
    ]
    out_specs = [
        pl.BlockSpec((nbt, lt, D_MODEL), lambda bi, ti: (bi, ti, 0)),
        pl.BlockSpec((nbt, CONV_HDR, CONV_DIM), lambda bi, ti: (bi, 0, 0)),
        pl.BlockSpec((nbt, SSD_WIDTH, D_STATE), lambda bi, ti: (bi, 0, 0)),
    ]
    out_shape = [
        jax.ShapeDtypeStruct((nb, seq, D_MODEL), F32),
        jax.ShapeDtypeStruct((nb, CONV_HDR, CONV_DIM), F32),
        jax.ShapeDtypeStruct((nb, SSD_WIDTH, D_STATE), F32),
    ]
    if emit_v:
        out_specs.append(pl.BlockSpec((nbt, lt, CMLP_WIDTH), lambda bi, ti: (bi, ti, 0)))
        out_shape.append(jax.ShapeDtypeStruct((nb, seq, CMLP_WIDTH), F32))
    scratch = [
        pltpu.VMEM((nbt, lt + CONV_HDR, CONV_DIM), F32),
        pltpu.VMEM((tm, SSD_WIDTH), F32),
        pltpu.VMEM((tm, LANES), F32),
        pltpu.VMEM((tm, CMLP_WIDTH), F32),
        pltpu.VMEM((tm, CMLP_WIDTH), F32),
        pltpu.VMEM((tm, D_MODEL), BF16),
        pltpu.VMEM((SSD_GROUPS, D_STATE, GROUP_W), F32),
    ]
    return pl.pallas_call(
        functools.partial(_mixer_kernel, nbt=nbt, lt=lt, ch=ch, emit_v=emit_v),
        grid=grid,
        in_specs=in_specs,
        out_specs=out_specs,
        out_shape=out_shape,
        scratch_shapes=scratch,
        compiler_params=pltpu.CompilerParams(
            dimension_semantics=("arbitrary", "arbitrary"), vmem_limit_bytes=VMEM_LIMIT),
        name="mixer",
    )(x, mod, cinit, sinit, ng, win, cw, cb, dtb, alog, dskip, sng, vg, vb, ws, bs, e3, wout)


FF_CHUNK = 1024


def _ffn_kernel(x_ref, mod_ref, ng_ref, w1_ref, w2_ref, o_ref, *, nbt, lt):
    tm = nbt * lt

    def rows(m):
        if nbt == 1:
            return m
        return jnp.broadcast_to(m[:, None, :], (nbt, lt, m.shape[-1])).reshape(tm, m.shape[-1])

    x = x_ref[...].reshape(tm, D_MODEL)
    mod = mod_ref[...]
    sh2, sc2, g2 = mod[:, 3, :], mod[:, 4, :], mod[:, 5, :]
    h = _rms(x) * rows(ng_ref[2:3, :] * (1.0 + sc2)) + rows(sh2)
    hb = h.astype(BF16)
    f = jnp.zeros((tm, D_MODEL), F32)
    for k in range(D_FF // FF_CHUNK):
        a = jnp.maximum(_dot(hb, w1_ref[:, k * FF_CHUNK:(k + 1) * FF_CHUNK]), 0.0)
        f = f + _dot((a * a).astype(BF16), w2_ref[k * FF_CHUNK:(k + 1) * FF_CHUNK, :])
    o = x + rows(g2) * (_rms(f) * ng_ref[3:4, :])
    o_ref[...] = o.reshape(nbt, lt, D_MODEL)


def _ffn_call(x, mod, ng, w1, w2, *, nbt, lt):
    nb, seq, _ = x.shape
    return pl.pallas_call(
        functools.partial(_ffn_kernel, nbt=nbt, lt=lt),
        grid=(nb // nbt, seq // lt),
        in_specs=[
            pl.BlockSpec((nbt, lt, D_MODEL), lambda bi, ti: (bi, ti, 0)),
            pl.BlockSpec((nbt, N_MOD, D_MODEL), lambda bi, ti: (bi, 0, 0)),
            pl.BlockSpec((4, D_MODEL), lambda bi, ti: (0, 0)),
            pl.BlockSpec((D_MODEL, D_FF), lambda bi, ti: (0, 0)),
            pl.BlockSpec((D_FF, D_MODEL), lambda bi, ti: (0, 0)),
        ],
        out_specs=pl.BlockSpec((nbt, lt, D_MODEL), lambda bi, ti: (bi, ti, 0)),
        out_shape=jax.ShapeDtypeStruct((nb, seq, D_MODEL), F32),
        compiler_params=pltpu.CompilerParams(
            dimension_semantics=("arbitrary", "arbitrary"), vmem_limit_bytes=VMEM_LIMIT),
        name="ffn",
    )(x, mod, ng, w1, w2)


def _expand_matrix():
    e = np.zeros((2 * LANES, 2 * SSD_WIDTH), np.float32)
    for blk in range(2):
        for l in range(3 * SSD_HEADS):
            hd = l % SSD_HEADS
            e[blk * LANES + l, blk * SSD_WIDTH + hd * SSD_HEAD_DIM:blk * SSD_WIDTH + (hd + 1) * SSD_HEAD_DIM] = 1.0
    return jnp.asarray(e, BF16)


def _tile_plan(nb, seq):
    if seq >= 512:
        return 1, 512, CMLP_CHUNK
    assert seq <= CMLP_CHUNK and seq % 16 == 0 and (nb * seq) % 8 == 0
    return nb, seq, seq


def _layer_params(l, w_in, w_out, w_ff1, w_ff2, conv_w, conv_b, dt_bias, a_log, d_skip,
                  ssd_norm_g, v_ln_g, v_ln_b):
    wi = w_in[l]
    z0, x0, d0, u0 = 0, SSD_WIDTH, SSD_WIDTH + CONV_DIM, SSD_WIDTH + CONV_DIM + SSD_HEADS
    reps = LANES // SSD_HEADS
    win = jnp.concatenate([
        wi[:, z0:x0], wi[:, x0:d0], wi[:, u0:u0 + CMLP_WIDTH], wi[:, u0 + CMLP_WIDTH:],
        jnp.tile(wi[:, d0:u0], (1, reps))], axis=1).astype(BF16)
    return dict(
        win=win, wout=w_out[l].astype(BF16), w1=w_ff1[l].astype(BF16), w2=w_ff2[l].astype(BF16),
        cw=conv_w[l], cb=conv_b[l].reshape(1, CONV_DIM),
        dtb=jnp.tile(dt_bias[l], reps).reshape(1, LANES),
        alog=jnp.tile(a_log[l], reps).reshape(1, LANES),
        dskip=jnp.repeat(d_skip[l], SSD_HEAD_DIM).reshape(1, SSD_WIDTH),
        sng=ssd_norm_g[l].reshape(1, SSD_WIDTH),
        vg=v_ln_g[l].reshape(1, CMLP_WIDTH), vb=v_ln_b[l].reshape(1, CMLP_WIDTH))


def _block(x, mod, cinit, sinit, p, ng, ws_l, bs_l, e3, emit_v):
    nb, seq, _ = x.shape
    nbt, lt, ch = _tile_plan(nb, seq)
    ws = ws_l[:, :ch, :ch]
    bs = jnp.repeat(bs_l[:, :ch].T, CMLP_GDIM, axis=1)
    cin8 = jnp.pad(cinit, ((0, 0), (CONV_HDR - (CONV_W - 1), 0), (0, 0)))
    outs = _mixer_call(x, mod, cin8, sinit.reshape(nb, SSD_WIDTH, D_STATE), ng, p["win"], p["cw"], p["cb"],
                       p["dtb"], p["alog"], p["dskip"], p["sng"], p["vg"], p["vb"], ws, bs, e3, p["wout"],
                       nbt=nbt, lt=lt, ch=ch, emit_v=emit_v)
    x1, cout, sout = outs[:3]
    x2 = _ffn_call(x1, mod, ng, p["w1"], p["w2"], nbt=nbt, lt=lt)
    conv_state = cout[:, CONV_HDR - (CONV_W - 1):, :]
    ssd_state = sout.reshape(nb, SSD_HEADS, SSD_HEAD_DIM, D_STATE)
    return x2, conv_state, ssd_state, (outs[3] if emit_v else None)


def kernel(x_prompt, x_sample, state_conv, state_ssd, c_prompt, c_sample, w_mod, b_mod, norm_g, w_in, conv_w,
           conv_b, dt_bias, a_log, d_skip, ssd_norm_g, v_ln_g, v_ln_b, w_s, b_s, w_out, w_ff1, w_ff2):
    bp = x_prompt.shape[0]
    bs_ = x_sample.shape[0]
    e3 = _expand_matrix()
    mod = _mod_call(jnp.concatenate([c_prompt, c_sample], axis=0), w_mod, b_mod)
    mod = mod.reshape(DEPTH, bp + bs_, N_MOD, D_MODEL)
    conv_p0 = jnp.zeros((bp, CONV_W - 1, CONV_DIM), F32)
    ssd_p0 = jnp.zeros((bp, SSD_HEADS, SSD_HEAD_DIM, D_STATE), F32)
    xp, xs = x_prompt, x_sample
    conv_p, ssd_p, conv_s, ssd_s, v_s = [], [], [], [], []
    for l in range(DEPTH):
        p = _layer_params(l, w_in, w_out, w_ff1, w_ff2, conv_w, conv_b, dt_bias, a_log, d_skip,
                          ssd_norm_g, v_ln_g, v_ln_b)
        xp, cp, hp, _ = _block(xp, mod[l, :bp], conv_p0, ssd_p0, p, norm_g[l], w_s[l], b_s[l], e3, False)
        xs, cs, hs, vs = _block(xs, mod[l, bp:], state_conv[l], state_ssd[l], p, norm_g[l], w_s[l], b_s[l], e3, True)
        conv_p.append(cp)
        ssd_p.append(hp)
        conv_s.append(cs)
        ssd_s.append(hs)
        v_s.append(vs)
    return (xp, xs, jnp.stack(conv_p), jnp.stack(ssd_p), jnp.stack(conv_s), jnp.stack(ssd_s), jnp.stack(v_s))
```

```python
import functools

import numpy as np
import jax
import jax.numpy as jnp
from jax import lax
from jax.experimental import pallas as pl
from jax.experimental.pallas import tpu as pltpu

F32 = jnp.float32
BF16 = jnp.bfloat16

D_MODEL = 1024
DEPTH = 4
SSD_WIDTH = 512
SSD_HEAD_DIM = 64
SSD_HEADS = 8
SSD_GROUPS = 2
SSD_HPG = 4
D_STATE = 128
CONV_W = 4
CONV_DIM = 1024
CMLP_WIDTH = 512
CMLP_GROUPS = 4
CMLP_GDIM = 128
CMLP_CHUNK = 128
D_FF = 4096
N_MOD = 6
RMS_EPS = 1e-6
LN_EPS = 1e-5

LANES = 128
CONV_HDR = 8
GROUP_W = SSD_HPG * SSD_HEAD_DIM
OFF_Z, OFF_XBC, OFF_U, OFF_V, OFF_DT = 0, 512, 1536, 2048, 2560
PROJ_W = OFF_DT + LANES
VMEM_LIMIT = 56 * 1024 * 1024


def _sigmoid(x):
    return 1.0 / (1.0 + jnp.exp(-x))


def _silu(x):
    return x * _sigmoid(x)


def _rms(x, eps=RMS_EPS):
    return x * lax.rsqrt(jnp.mean(x * x, axis=-1, keepdims=True) + eps)


def _dot(a, b):
    return jnp.dot(a, b, preferred_element_type=F32)


def _mod_kernel(c_ref, w_ref, b_ref, o_ref):
    s = _silu(c_ref[...]).astype(BF16)
    o_ref[0] = _dot(s, w_ref[0].astype(BF16)) + b_ref[0]


def _mod_call(c_all, w_mod, b_mod):
    nb = c_all.shape[0]
    tn = 1536
    n = N_MOD * D_MODEL
    return pl.pallas_call(
        _mod_kernel,
        grid=(DEPTH, n // tn),
        in_specs=[
            pl.BlockSpec((nb, D_MODEL), lambda l, j: (0, 0)),
            pl.BlockSpec((1, D_MODEL, tn), lambda l, j: (l, 0, j)),
            pl.BlockSpec((1, 1, tn), lambda l, j: (l, 0, j)),
        ],
        out_specs=pl.BlockSpec((1, nb, tn), lambda l, j: (l, 0, j)),
        out_shape=jax.ShapeDtypeStruct((DEPTH, nb, n), F32),
        compiler_params=pltpu.CompilerParams(
            dimension_semantics=("arbitrary", "arbitrary"), vmem_limit_bytes=VMEM_LIMIT),
        name="mod",
    )(c_all, w_mod, b_mod.reshape(DEPTH, 1, n))


def _mixer_kernel(x_ref, mod_ref, cinit_ref, sinit_ref, ng_ref, win_ref, cw_ref, cb_ref,
                  dtb_ref, alog_ref, dskip_ref, sng_ref, vg_ref, vb_ref, ws_ref, bs_ref,
                  e2_ref, wout_ref,
                  *refs, nbt, lt, ch, nt, emit_v):
    if emit_v:
        x1_ref, cout_ref, sout_ref, vout_ref = refs[:4]
        scratch = refs[4:]
    else:
        x1_ref, cout_ref, sout_ref = refs[:3]
        vout_ref = None
        scratch = refs[3:]
    xbc_ref, z_ref, dt_ref, u_ref, v_ref, y_ref, s_ref = scratch

    t = pl.program_id(1)
    tm = nbt * lt
    nc = lt // ch

    def when_first_tile(fn):
        fn() if nt == 1 else pl.when(t == 0)(fn)

    def when_last_tile(fn):
        fn() if nt == 1 else pl.when(t == nt - 1)(fn)

    def rows(m):
        if nbt == 1:
            return m
        return jnp.broadcast_to(m[:, None, :], (nbt, lt, m.shape[-1])).reshape(tm, m.shape[-1])

    x = x_ref[...].reshape(tm, D_MODEL)
    mod = mod_ref[...]
    sh1, sc1, g1 = mod[:, 0, :], mod[:, 1, :], mod[:, 2, :]
    h = _rms(x) * rows(ng_ref[0:1, :] * (1.0 + sc1)) + rows(sh1)
    hb = h.astype(BF16)

    for b in range(nbt):
        def from_cache(b=b):
            xbc_ref[b, 0:CONV_HDR, :] = cinit_ref[b]

        def from_prev(b=b):
            xbc_ref[b, 0:CONV_HDR, :] = xbc_ref[b, lt:lt + CONV_HDR, :]

        when_first_tile(from_cache)
        if nt > 1:
            pl.when(t > 0)(from_prev)

    z_ref[...] = _silu(_dot(hb, win_ref[:, OFF_Z:OFF_XBC]))
    xbc = _dot(hb, win_ref[:, OFF_XBC:OFF_U])
    for b in range(nbt):
        xbc_ref[b, CONV_HDR:CONV_HDR + lt, :] = xbc[b * lt:(b + 1) * lt, :]
    u_ref[...] = jax.nn.gelu(_dot(hb, win_ref[:, OFF_U:OFF_V]), approximate=True)
    v_ref[...] = jax.nn.gelu(_dot(hb, win_ref[:, OFF_V:OFF_DT]), approximate=True)
    dt_ref[...] = _dot(hb, win_ref[:, OFF_DT:PROJ_W])

    row_i = lax.broadcasted_iota(jnp.int32, (ch, LANES), 0)
    lane_i = lax.broadcasted_iota(jnp.int32, (ch, LANES), 1)
    ii = lax.broadcasted_iota(jnp.int32, (ch, ch), 0)
    jj = lax.broadcasted_iota(jnp.int32, (ch, ch), 1)
    causal = ii >= jj
    head_of_lane = lax.broadcasted_iota(jnp.int32, (ch, GROUP_W), 1) // SSD_HEAD_DIM
    a_row = -jnp.exp(alog_ref[...])
    w_mix = [jnp.where(causal, ws_ref[g], 0.0).astype(BF16) for g in range(CMLP_GROUPS)]

    def pack2(v):
        hi = v.astype(BF16).astype(F32)
        mid = (v - hi).astype(BF16).astype(F32)
        return jnp.where(lane_i < 8, hi, jnp.where(lane_i < 16, mid, 0.0)).astype(BF16)

    def transpose_rows(v):
        if ch == LANES:
            return v.T
        return jnp.concatenate([v, jnp.zeros((LANES - ch, LANES), F32)], axis=0).T[:, :ch]

    for i in range(nbt * nc):
        b, c = divmod(i, nc)
        r0 = c * ch
        rs = slice(i * ch, (i + 1) * ch)

        big = xbc_ref[b, r0:r0 + ch + CONV_HDR, :]
        acc = cb_ref[...] + cw_ref[CONV_W - 1:CONV_W, :] * big[CONV_HDR:, :]
        for k in range(CONV_W - 1):
            acc = acc + cw_ref[k:k + 1, :] * pltpu.roll(big, CONV_W - 1 - k, 0)[CONV_HDR:, :]
        xbc_a = _silu(acc)
        xs = xbc_a[:, :SSD_WIDTH]
        xsb = xs.astype(BF16)
        bmb = xbc_a[:, SSD_WIDTH:SSD_WIDTH + SSD_GROUPS * D_STATE].astype(BF16)
        cmb = xbc_a[:, SSD_WIDTH + SSD_GROUPS * D_STATE:].astype(BF16)

        dtr = dt_ref[rs, :] + dtb_ref[...]
        dt = jnp.maximum(dtr, 0.0) + jnp.log1p(jnp.exp(-jnp.abs(dtr)))
        acum = dt * a_row
        s = 1
        while s < ch:
            acum = acum + jnp.where(row_i >= s, pltpu.roll(acum, s, 0), 0.0)
            s *= 2
        ea = jnp.exp(acum)
        wend = jnp.exp(acum[ch - 1:ch, :] - acum) * dt
        ex = _dot(jnp.concatenate([pack2(ea), pack2(wend)], axis=1), e2_ref[...])
        ea_e = ex[:, :SSD_WIDTH]
        xw = (xs * ex[:, SSD_WIDTH:]).astype(BF16)
        acum_t = transpose_rows(acum)
        dt_t = transpose_rows(dt)

        if c == 0:
            def load_state(b=b):
                for g in range(SSD_GROUPS):
                    s_ref[g] = sinit_ref[b, g * GROUP_W:(g + 1) * GROUP_W, :].T
            when_first_tile(load_state)

        ys = []
        for g in range(SSD_GROUPS):
            gs = slice(g * GROUP_W, (g + 1) * GROUP_W)
            cg = cmb[:, g * D_STATE:(g + 1) * D_STATE]
            bg = bmb[:, g * D_STATE:(g + 1) * D_STATE]
            cbm = lax.dot_general(cg, bg, (((1,), (1,)), ((), ())), preferred_element_type=F32)
            yg = None
            for r in range(SSD_HPG):
                hh = g * SSD_HPG + r
                dec = jnp.where(causal, jnp.exp(acum[:, hh:hh + 1] - acum_t[hh:hh + 1, :]), 0.0)
                mm = (cbm * (dec * dt_t[hh:hh + 1, :])).astype(BF16)
                rhs = jnp.where(head_of_lane == r, xsb[:, gs], jnp.zeros((ch, GROUP_W), BF16))
                part = _dot(mm, rhs)
                yg = part if yg is None else yg + part
            sg = s_ref[g]
            yg = yg + _dot(cg, sg.astype(BF16)) * ea_e[:, gs]
            ds = lax.dot_general(bg, xw[:, gs], (((0,), (0,)), ((), ())), preferred_element_type=F32)
            s_ref[g] = ea_e[ch - 1:ch, gs] * sg + ds
            ys.append(yg)
        y = jnp.concatenate(ys, axis=1) + dskip_ref[...] * xs
        y = _rms(y * z_ref[rs, :]) * sng_ref[...]
        y_ref[rs, 0:SSD_WIDTH] = y.astype(BF16)

        if c == nc - 1:
            def store_state(b=b):
                for g in range(SSD_GROUPS):
                    sout_ref[b, g * GROUP_W:(g + 1) * GROUP_W, :] = s_ref[g].T
                cout_ref[b] = xbc_ref[b, lt:lt + CONV_HDR, :]
            when_last_tile(store_state)

        u = u_ref[rs, :]
        v = v_ref[rs, :]
        vc = v - jnp.mean(v, axis=-1, keepdims=True)
        vn = vc * lax.rsqrt(jnp.mean(vc * vc, axis=-1, keepdims=True) + LN_EPS) * vg_ref[...] + vb_ref[...]
        if emit_v:
            vout_ref[b, r0:r0 + ch, :] = vn
        vnb = vn.astype(BF16)
        for g in range(CMLP_GROUPS):
            sl = slice(g * CMLP_GDIM, (g + 1) * CMLP_GDIM)
            mixed = _dot(w_mix[g], vnb[:, sl]) + bs_ref[:, sl]
            y_ref[rs, SSD_WIDTH + g * CMLP_GDIM:SSD_WIDTH + (g + 1) * CMLP_GDIM] = (u[:, sl] * mixed).astype(BF16)

    m = _dot(y_ref[...], wout_ref[...])
    x1 = x + rows(g1 * ng_ref[1:2, :]) * _rms(m)
    x1_ref[...] = x1.reshape(nbt, lt, D_MODEL)


def _mixer_call(x, mod, cinit, sinit, ng, win, cw, cb, dtb, alog, dskip, sng, vg, vb, ws, bs,
                e2, wout, *, nbt, lt, ch, emit_v):
    nb, seq, _ = x.shape
    tm = nbt * lt
    grid = (nb // nbt, seq // lt)

    def const(shape):
        return pl.BlockSpec(shape, lambda bi, ti: (0,) * len(shape))

    in_specs = [
        pl.BlockSpec((nbt, lt, D_MODEL), lambda bi, ti: (bi, ti, 0)),
        pl.BlockSpec((nbt, N_MOD, D_MODEL), lambda bi, ti: (bi, 0, 0)),
        pl.BlockSpec((nbt, CONV_HDR, CONV_DIM), lambda bi, ti: (bi, 0, 0)),
        pl.BlockSpec((nbt, SSD_WIDTH, D_STATE), lambda bi, ti: (bi, 0, 0)),
        const((4, D_MODEL)),
        const((D_MODEL, PROJ_W)),
        const((CONV_W, CONV_DIM)),
        const((1, CONV_DIM)),
        const((1, LANES)),
        const((1, LANES)),
        const((1, SSD_WIDTH)),
        const((1, SSD_WIDTH)),
        const((1, CMLP_WIDTH)),
        const((1, CMLP_WIDTH)),
        const((CMLP_GROUPS, ch, ch)),
        const((ch, CMLP_WIDTH)),
        const((2 * LANES, 2 * SSD_WIDTH)),
        const((D_MODEL, D_MODEL)),
    ]
    out_specs = [
        pl.BlockSpec((nbt, lt, D_MODEL), lambda bi, ti: (bi, ti, 0)),
        pl.BlockSpec((nbt, CONV_HDR, CONV_DIM), lambda bi, ti: (bi, 0, 0)),
        pl.BlockSpec((nbt, SSD_WIDTH, D_STATE), lambda bi, ti: (bi, 0, 0)),
    ]
    out_shape = [
        jax.ShapeDtypeStruct((nb, seq, D_MODEL), F32),
        jax.ShapeDtypeStruct((nb, CONV_HDR, CONV_DIM), F32),
        jax.ShapeDtypeStruct((nb, SSD_WIDTH, D_STATE), F32),
    ]
    if emit_v:
        out_specs.append(pl.BlockSpec((nbt, lt, CMLP_WIDTH), lambda bi, ti: (bi, ti, 0)))
        out_shape.append(jax.ShapeDtypeStruct((nb, seq, CMLP_WIDTH), F32))
    scratch = [
        pltpu.VMEM((nbt, lt + CONV_HDR, CONV_DIM), F32),
        pltpu.VMEM((tm, SSD_WIDTH), F32),
        pltpu.VMEM((tm, LANES), F32),
        pltpu.VMEM((tm, CMLP_WIDTH), F32),
        pltpu.VMEM((tm, CMLP_WIDTH), F32),
        pltpu.VMEM((tm, D_MODEL), BF16),
        pltpu.VMEM((SSD_GROUPS, D_STATE, GROUP_W), F32),
    ]
    return pl.pallas_call(
        functools.partial(_mixer_kernel, nbt=nbt, lt=lt, ch=ch, nt=grid[1], emit_v=emit_v),
        grid=grid,
        in_specs=in_specs,
        out_specs=out_specs,
        out_shape=out_shape,
        scratch_shapes=scratch,
        compiler_params=pltpu.CompilerParams(
            dimension_semantics=("arbitrary", "arbitrary"), vmem_limit_bytes=VMEM_LIMIT),
        name="mixer",
    )(x, mod, cinit, sinit, ng, win, cw, cb, dtb, alog, dskip, sng, vg, vb, ws, bs, e2, wout)


FF_CHUNK = 1024


def _ffn_kernel(x_ref, mod_ref, ng_ref, w1_ref, w2_ref, o_ref, *, nbt, lt):
    tm = nbt * lt

    def rows(m):
        if nbt == 1:
            return m
        return jnp.broadcast_to(m[:, None, :], (nbt, lt, m.shape[-1])).reshape(tm, m.shape[-1])

    x = x_ref[...].reshape(tm, D_MODEL)
    mod = mod_ref[...]
    sh2, sc2, g2 = mod[:, 3, :], mod[:, 4, :], mod[:, 5, :]
    h = _rms(x) * rows(ng_ref[2:3, :] * (1.0 + sc2)) + rows(sh2)
    hb = h.astype(BF16)
    f = jnp.zeros((tm, D_MODEL), F32)
    for k in range(D_FF // FF_CHUNK):
        a = jnp.maximum(_dot(hb, w1_ref[:, k * FF_CHUNK:(k + 1) * FF_CHUNK]), 0.0)
        f = f + _dot((a * a).astype(BF16), w2_ref[k * FF_CHUNK:(k + 1) * FF_CHUNK, :])
    o = x + rows(g2) * (_rms(f) * ng_ref[3:4, :])
    o_ref[...] = o.reshape(nbt, lt, D_MODEL)


def _ffn_call(x, mod, ng, w1, w2, *, nbt, lt):
    nb, seq, _ = x.shape
    return pl.pallas_call(
        functools.partial(_ffn_kernel, nbt=nbt, lt=lt),
        grid=(nb // nbt, seq // lt),
        in_specs=[
            pl.BlockSpec((nbt, lt, D_MODEL), lambda bi, ti: (bi, ti, 0)),
            pl.BlockSpec((nbt, N_MOD, D_MODEL), lambda bi, ti: (bi, 0, 0)),
            pl.BlockSpec((4, D_MODEL), lambda bi, ti: (0, 0)),
            pl.BlockSpec((D_MODEL, D_FF), lambda bi, ti: (0, 0)),
            pl.BlockSpec((D_FF, D_MODEL), lambda bi, ti: (0, 0)),
        ],
        out_specs=pl.BlockSpec((nbt, lt, D_MODEL), lambda bi, ti: (bi, ti, 0)),
        out_shape=jax.ShapeDtypeStruct((nb, seq, D_MODEL), F32),
        compiler_params=pltpu.CompilerParams(
            dimension_semantics=("arbitrary", "arbitrary"), vmem_limit_bytes=VMEM_LIMIT),
        name="ffn",
    )(x, mod, ng, w1, w2)


def _expand_matrix():
    e = np.zeros((2 * LANES, 2 * SSD_WIDTH), np.float32)
    for blk in range(2):
        for l in range(2 * SSD_HEADS):
            hd = l % SSD_HEADS
            e[blk * LANES + l, blk * SSD_WIDTH + hd * SSD_HEAD_DIM:blk * SSD_WIDTH + (hd + 1) * SSD_HEAD_DIM] = 1.0
    return jnp.asarray(e, BF16)


def _tile_plan(nb, seq):
    if seq >= 512:
        return 1, 512, CMLP_CHUNK
    assert seq <= CMLP_CHUNK and seq % 16 == 0 and (nb * seq) % 8 == 0
    return nb, seq, seq


def _layer_params(l, w_in, w_out, w_ff1, w_ff2, conv_w, conv_b, dt_bias, a_log, d_skip,
                  ssd_norm_g, v_ln_g, v_ln_b):
    wi = w_in[l]
    z0, x0, d0, u0 = 0, SSD_WIDTH, SSD_WIDTH + CONV_DIM, SSD_WIDTH + CONV_DIM + SSD_HEADS
    reps = LANES // SSD_HEADS
    win = jnp.concatenate([
        wi[:, z0:x0], wi[:, x0:d0], wi[:, u0:u0 + CMLP_WIDTH], wi[:, u0 + CMLP_WIDTH:],
        jnp.tile(wi[:, d0:u0], (1, reps))], axis=1).astype(BF16)
    return dict(
        win=win, wout=w_out[l].astype(BF16), w1=w_ff1[l].astype(BF16), w2=w_ff2[l].astype(BF16),
        cw=conv_w[l], cb=conv_b[l].reshape(1, CONV_DIM),
        dtb=jnp.tile(dt_bias[l], reps).reshape(1, LANES),
        alog=jnp.tile(a_log[l], reps).reshape(1, LANES),
        dskip=jnp.repeat(d_skip[l], SSD_HEAD_DIM).reshape(1, SSD_WIDTH),
        sng=ssd_norm_g[l].reshape(1, SSD_WIDTH),
        vg=v_ln_g[l].reshape(1, CMLP_WIDTH), vb=v_ln_b[l].reshape(1, CMLP_WIDTH))


def _block(x, mod, cinit, sinit, p, ng, ws_l, bs_l, e2, emit_v):
    nb, seq, _ = x.shape
    nbt, lt, ch = _tile_plan(nb, seq)
    ws = ws_l[:, :ch, :ch]
    bs = jnp.repeat(bs_l[:, :ch].T, CMLP_GDIM, axis=1)
    cin8 = jnp.pad(cinit, ((0, 0), (CONV_HDR - (CONV_W - 1), 0), (0, 0)))
    outs = _mixer_call(x, mod, cin8, sinit.reshape(nb, SSD_WIDTH, D_STATE), ng, p["win"], p["cw"], p["cb"],
                       p["dtb"], p["alog"], p["dskip"], p["sng"], p["vg"], p["vb"], ws, bs, e2, p["wout"],
                       nbt=nbt, lt=lt, ch=ch, emit_v=emit_v)
    x1, cout, sout = outs[:3]
    x2 = _ffn_call(x1, mod, ng, p["w1"], p["w2"], nbt=nbt, lt=lt)
    conv_state = cout[:, CONV_HDR - (CONV_W - 1):, :]
    ssd_state = sout.reshape(nb, SSD_HEADS, SSD_HEAD_DIM, D_STATE)
    return x2, conv_state, ssd_state, (outs[3] if emit_v else None)


def kernel(x_prompt, x_sample, state_conv, state_ssd, c_prompt, c_sample, w_mod, b_mod, norm_g, w_in, conv_w,
           conv_b, dt_bias, a_log, d_skip, ssd_norm_g, v_ln_g, v_ln_b, w_s, b_s, w_out, w_ff1, w_ff2):
    bp = x_prompt.shape[0]
    bs_ = x_sample.shape[0]
    e2 = _expand_matrix()
    mod = _mod_call(jnp.concatenate([c_prompt, c_sample], axis=0), w_mod, b_mod)
    mod = mod.reshape(DEPTH, bp + bs_, N_MOD, D_MODEL)
    conv_p0 = jnp.zeros((bp, CONV_W - 1, CONV_DIM), F32)
    ssd_p0 = jnp.zeros((bp, SSD_HEADS, SSD_HEAD_DIM, D_STATE), F32)
    xp, xs = x_prompt, x_sample
    conv_p, ssd_p, conv_s, ssd_s, v_s = [], [], [], [], []
    for l in range(DEPTH):
        p = _layer_params(l, w_in, w_out, w_ff1, w_ff2, conv_w, conv_b, dt_bias, a_log, d_skip,
                          ssd_norm_g, v_ln_g, v_ln_b)
        xp, cp, hp, _ = _block(xp, mod[l, :bp], conv_p0, ssd_p0, p, norm_g[l], w_s[l], b_s[l], e2, False)
        xs, cs, hs, vs = _block(xs, mod[l, bp:], state_conv[l], state_ssd[l], p, norm_g[l], w_s[l], b_s[l], e2, True)
        conv_p.append(cp)
        ssd_p.append(hp)
        conv_s.append(cs)
        ssd_s.append(hs)
        v_s.append(vs)
    return (xp, xs, jnp.stack(conv_p), jnp.stack(ssd_p), jnp.stack(conv_s), jnp.stack(ssd_s), jnp.stack(v_s))
```

```python
import functools

import numpy as np
import jax
import jax.numpy as jnp
from jax import lax
from jax.experimental import pallas as pl
from jax.experimental.pallas import tpu as pltpu

F32 = jnp.float32
BF16 = jnp.bfloat16

D_MODEL = 1024
DEPTH = 4
SSD_WIDTH = 512
SSD_HEAD_DIM = 64
SSD_HEADS = 8
SSD_GROUPS = 2
SSD_HPG = 4
D_STATE = 128
CONV_W = 4
CONV_DIM = 1024
CMLP_WIDTH = 512
CMLP_GROUPS = 4
CMLP_GDIM = 128
CMLP_CHUNK = 128
D_FF = 4096
N_MOD = 6
RMS_EPS = 1e-6
LN_EPS = 1e-5

LANES = 128
CONV_HDR = 8
GROUP_W = SSD_HPG * SSD_HEAD_DIM
OFF_Z, OFF_XBC, OFF_U, OFF_V, OFF_DT = 0, 512, 1536, 2048, 2560
PROJ_W = OFF_DT + LANES
VMEM_LIMIT = 56 * 1024 * 1024


def _silu(x):
    h = 0.5 * x
    return h + h * jnp.tanh(h)


GELU_C0 = float(np.sqrt(2.0 / np.pi))
GELU_C1 = GELU_C0 * 0.044715


def _gelu_tanh(x):
    h = 0.5 * x
    return h + h * jnp.tanh(x * (GELU_C0 + GELU_C1 * (x * x)))


def _rms(x, eps=RMS_EPS):
    return x * lax.rsqrt(jnp.mean(x * x, axis=-1, keepdims=True) + eps)


def _dot(a, b):
    return jnp.dot(a, b, preferred_element_type=F32)


def _mod_kernel(c_ref, w_ref, b_ref, o_ref):
    s = _silu(c_ref[...]).astype(BF16)
    o_ref[0] = _dot(s, w_ref[0].astype(BF16)) + b_ref[0]


def _mod_call(c_all, w_mod, b_mod):
    nb = c_all.shape[0]
    tn = 1536
    n = N_MOD * D_MODEL
    return pl.pallas_call(
        _mod_kernel,
        grid=(DEPTH, n // tn),
        in_specs=[
            pl.BlockSpec((nb, D_MODEL), lambda l, j: (0, 0)),
            pl.BlockSpec((1, D_MODEL, tn), lambda l, j: (l, 0, j)),
            pl.BlockSpec((1, 1, tn), lambda l, j: (l, 0, j)),
        ],
        out_specs=pl.BlockSpec((1, nb, tn), lambda l, j: (l, 0, j)),
        out_shape=jax.ShapeDtypeStruct((DEPTH, nb, n), F32),
        compiler_params=pltpu.CompilerParams(
            dimension_semantics=("arbitrary", "arbitrary"), vmem_limit_bytes=VMEM_LIMIT),
        name="mod",
    )(c_all, w_mod, b_mod.reshape(DEPTH, 1, n))


N_SLOT_BUFS = 6


def _mixer_kernel(xa_ref, xc_ref, moda_ref, modc_ref, cinit_ref, sinit_ref, ng_ref, win_ref, cw_ref, cb_ref,
                  dtb_ref, alog_ref, dskip_ref, sng_ref, vg_ref, vb_ref, ws_ref, bs_ref,
                  e2_ref, wout_ref,
                  *refs, nbt, lt, ch, nt, n_tiles, pipelined, emit_v):
    n_out = 4 if emit_v else 3
    x1_ref, cout_ref, sout_ref = refs[:3]
    vout_ref = refs[3] if emit_v else None
    scratch = refs[n_out:]
    slots = [scratch[k * N_SLOT_BUFS:(k + 1) * N_SLOT_BUFS] for k in range(2 if pipelined else 1)]
    s_ref, = scratch[len(slots) * N_SLOT_BUFS:]
    stack_k = ch == LANES

    s_id = pl.program_id(0)
    tm = nbt * lt
    nc = lt // ch

    def rows(m):
        if nbt == 1:
            return m
        return jnp.broadcast_to(m[:, None, :], (nbt, lt, m.shape[-1])).reshape(tm, m.shape[-1])

    n_sub = ch // 8
    sub_i = lax.broadcasted_iota(jnp.int32, (n_sub, 8, LANES), 1)
    lane_i = lax.broadcasted_iota(jnp.int32, (ch, LANES), 1)
    ii = lax.broadcasted_iota(jnp.int32, (ch, ch), 0)
    jj = lax.broadcasted_iota(jnp.int32, (ch, ch), 1)
    causal = ii >= jj
    head_of_lane = lax.broadcasted_iota(jnp.int32, (ch, GROUP_W), 1) // SSD_HEAD_DIM

    def pack2(v):
        hi = v.astype(BF16).astype(F32)
        mid = (v - hi).astype(BF16).astype(F32)
        return jnp.where(lane_i < 8, hi, jnp.where(lane_i < 16, mid, 0.0)).astype(BF16)

    def transpose_rows(v):
        if ch == LANES:
            return v.T
        return jnp.concatenate([v, jnp.zeros((LANES - ch, LANES), F32)], axis=0).T[:, :ch]

    def cumsum_rows(a):
        a3 = a.reshape(n_sub, 8, LANES)
        for sft in (1, 2, 4):
            a3 = a3 + jnp.where(sub_i >= sft, pltpu.roll(a3, sft, 1), 0.0)
        carry = jnp.zeros((1, LANES), F32)
        out = []
        for k in range(n_sub):
            out.append(a3[k] + carry)
            carry = carry + a3[k, 7:8, :]
        return jnp.concatenate(out, axis=0)

    def project(slot):
        xbc_ref, z_ref, dt_ref, u_ref, v_ref, _ = slot
        x = xa_ref[...].reshape(tm, D_MODEL)
        mod = moda_ref[...]
        sh1, sc1 = mod[:, 0, :], mod[:, 1, :]
        h = _rms(x) * rows(ng_ref[0:1, :] * (1.0 + sc1)) + rows(sh1)
        hb = h.astype(BF16)
        yield
        z_ref[...] = _silu(_dot(hb, win_ref[:, OFF_Z:OFF_XBC]))
        yield
        xbc = _dot(hb, win_ref[:, OFF_XBC:OFF_U])
        for b in range(nbt):
            xbc_ref[b, CONV_HDR:CONV_HDR + lt, :] = xbc[b * lt:(b + 1) * lt, :]
        yield
        u_ref[...] = _gelu_tanh(_dot(hb, win_ref[:, OFF_U:OFF_V]))
        yield
        v_ref[...] = _gelu_tanh(_dot(hb, win_ref[:, OFF_V:OFF_DT]))
        yield
        dt_ref[...] = _dot(hb, win_ref[:, OFF_DT:PROJ_W])

    def init_tile(slot):
        xbc_ref = slot[0]
        for b in range(nbt):
            xbc_ref[b, 0:CONV_HDR, :] = cinit_ref[b]
        if nbt == 1:
            for g in range(SSD_GROUPS):
                s_ref[g] = sinit_ref[0, g * GROUP_W:(g + 1) * GROUP_W, :].T

    def finish_tile(slot):
        xbc_ref = slot[0]
        for b in range(nbt):
            cout_ref[b] = xbc_ref[b, lt:lt + CONV_HDR, :]
        if nbt == 1:
            for g in range(SSD_GROUPS):
                sout_ref[0, g * GROUP_W:(g + 1) * GROUP_W, :] = s_ref[g].T

    def mix(slot, next_slot):
        xbc_ref, z_ref, dt_ref, u_ref, v_ref, y_ref = slot
        a_row = -jnp.exp(alog_ref[...])
        w_mix = [jnp.where(causal, ws_ref[g], 0.0).astype(BF16) for g in range(CMLP_GROUPS)]
        if stack_k:
            w_mix = [jnp.concatenate(w_mix[2 * q:2 * q + 2], axis=1) for q in range(CMLP_GROUPS // 2)]
        zeros_gd = jnp.zeros((ch, CMLP_GDIM), BF16)
        for i in range(nbt * nc):
            b, c = divmod(i, nc)
            r0 = c * ch
            rs = slice(i * ch, (i + 1) * ch)

            big = xbc_ref[b, r0:r0 + ch + CONV_HDR, :]
            acc = cb_ref[...] + cw_ref[CONV_W - 1:CONV_W, :] * big[CONV_HDR:, :]
            for k in range(CONV_W - 1):
                acc = acc + cw_ref[k:k + 1, :] * pltpu.roll(big, CONV_W - 1 - k, 0)[CONV_HDR:, :]
            xbc_a = _silu(acc)
            xs = xbc_a[:, :SSD_WIDTH]
            xsb = xs.astype(BF16)
            bmb = xbc_a[:, SSD_WIDTH:SSD_WIDTH + SSD_GROUPS * D_STATE].astype(BF16)
            cmb = xbc_a[:, SSD_WIDTH + SSD_GROUPS * D_STATE:].astype(BF16)

            dtr = dt_ref[rs, :] + dtb_ref[...]
            dt = jnp.maximum(dtr, 0.0) + jnp.log1p(jnp.exp(-jnp.abs(dtr)))
            acum = cumsum_rows(dt * a_row)
            ea = jnp.exp(acum)
            wend = jnp.exp(acum[ch - 1:ch, :] - acum) * dt
            ex = _dot(jnp.concatenate([pack2(ea), pack2(wend)], axis=1), e2_ref[...])
            ea_e = ex[:, :SSD_WIDTH]
            xw = (xs * ex[:, SSD_WIDTH:]).astype(BF16)
            acum_t = transpose_rows(acum)
            dt_t = transpose_rows(dt)

            if nbt > 1:
                for g in range(SSD_GROUPS):
                    s_ref[g] = sinit_ref[b, g * GROUP_W:(g + 1) * GROUP_W, :].T

            ys = []
            for g in range(SSD_GROUPS):
                gs = slice(g * GROUP_W, (g + 1) * GROUP_W)
                cg = cmb[:, g * D_STATE:(g + 1) * D_STATE]
                bg = bmb[:, g * D_STATE:(g + 1) * D_STATE]
                cbm = lax.dot_general(cg, bg, (((1,), (1,)), ((), ())), preferred_element_type=F32)
                mms, rhss = [], []
                for r in range(SSD_HPG):
                    hh = g * SSD_HPG + r
                    dec = jnp.where(causal, jnp.exp(acum[:, hh:hh + 1] - acum_t[hh:hh + 1, :]), 0.0)
                    mms.append((cbm * (dec * dt_t[hh:hh + 1, :])).astype(BF16))
                    rhss.append(jnp.where(head_of_lane == r, xsb[:, gs], jnp.zeros((ch, GROUP_W), BF16)))
                if stack_k:
                    mms = [jnp.concatenate(mms[2 * q:2 * q + 2], axis=1) for q in range(SSD_HPG // 2)]
                    rhss = [jnp.concatenate(rhss[2 * q:2 * q + 2], axis=0) for q in range(SSD_HPG // 2)]
                yg = None
                for mm, rhs in zip(mms, rhss):
                    part = _dot(mm, rhs)
                    yg = part if yg is None else yg + part
                sg = s_ref[g]
                yg = yg + _dot(cg, sg.astype(BF16)) * ea_e[:, gs]
                ds = lax.dot_general(bg, xw[:, gs], (((0,), (0,)), ((), ())), preferred_element_type=F32)
                s_ref[g] = ea_e[ch - 1:ch, gs] * sg + ds
                ys.append(yg)
            y = jnp.concatenate(ys, axis=1) + dskip_ref[...] * xs
            y = _rms(y * z_ref[rs, :]) * sng_ref[...]
            y_ref[rs, 0:SSD_WIDTH] = y.astype(BF16)

            if nbt > 1:
                for g in range(SSD_GROUPS):
                    sout_ref[b, g * GROUP_W:(g + 1) * GROUP_W, :] = s_ref[g].T

            u = u_ref[rs, :]
            v = v_ref[rs, :]
            vc = v - jnp.mean(v, axis=-1, keepdims=True)
            vn = vc * lax.rsqrt(jnp.mean(vc * vc, axis=-1, keepdims=True) + LN_EPS) * vg_ref[...] + vb_ref[...]
            if emit_v:
                vout_ref[b, r0:r0 + ch, :] = vn
            vnb = vn.astype(BF16)
            gw = 2 * CMLP_GDIM if stack_k else CMLP_GDIM
            for q, wq in enumerate(w_mix):
                sl = slice(q * gw, (q + 1) * gw)
                if stack_k:
                    va, vb_ = vnb[:, q * gw:q * gw + CMLP_GDIM], vnb[:, q * gw + CMLP_GDIM:(q + 1) * gw]
                    rhs = jnp.concatenate([jnp.concatenate([va, zeros_gd], axis=1),
                                           jnp.concatenate([zeros_gd, vb_], axis=1)], axis=0)
                else:
                    rhs = vnb[:, sl]
                mixed = _dot(wq, rhs) + bs_ref[:, sl]
                y_ref[rs, SSD_WIDTH + q * gw:SSD_WIDTH + (q + 1) * gw] = (u[:, sl] * mixed).astype(BF16)
            yield

        for b in range(nbt):
            next_slot[0][b, 0:CONV_HDR, :] = xbc_ref[b, lt:lt + CONV_HDR, :]

    def emit(slot):
        y_ref = slot[5]
        x = xc_ref[...].reshape(tm, D_MODEL)
        g1 = modc_ref[...][:, 2, :]
        m = _dot(y_ref[...], wout_ref[...])
        yield
        x1 = x + rows(g1 * ng_ref[1:2, :]) * _rms(m)
        x1_ref[...] = x1.reshape(nbt, lt, D_MODEL)

    def run(gens, order):
        for k in order:
            next(gens[k], None)
        for g in gens:
            for _ in g:
                pass

    if not pipelined:
        assert nt == 1
        init_tile(slots[0])
        run([project(slots[0]), mix(slots[0], slots[0]), emit(slots[0])], ())
        finish_tile(slots[0])
        return

    assert nbt == 1
    tb = jnp.clip(s_id - 1, 0, n_tiles - 1) % nt
    mix_is_real = jnp.logical_and(s_id >= 1, s_id <= n_tiles)

    @pl.when(s_id == 0)
    def _():
        for ref in slots[1]:
            ref[...] = jnp.zeros(ref.shape, ref.dtype)
        slots[0][5][...] = jnp.zeros(slots[0][5].shape, BF16)

    for par in range(2):
        @pl.when(s_id % 2 == par)
        def _(par=par):
            mine, other = slots[1 - par], slots[par]
            pl.when(tb == 0)(lambda: init_tile(mine))
            run([project(other), mix(mine, other), emit(other)], (2, 0, 1, 0, 0, 2, 1, 0, 1, 0, 1, 0))
            pl.when(jnp.logical_and(tb == nt - 1, mix_is_real))(lambda: finish_tile(mine))


def _mixer_call(x, mod, cinit, sinit, ng, win, cw, cb, dtb, alog, dskip, sng, vg, vb, ws, bs,
                e2, wout, *, nbt, lt, ch, emit_v):
    nb, seq, _ = x.shape
    tm = nbt * lt
    nt = seq // lt
    n_tiles = (nb // nbt) * nt
    pipelined = n_tiles > 1
    lag = 1 if pipelined else 0
    n_steps = n_tiles + 2 * lag

    def tile_a(s):
        return jnp.clip(s, 0, n_tiles - 1)

    def tile_b(s):
        return jnp.clip(s - lag, 0, n_tiles - 1)

    def tile_c(s):
        return jnp.clip(s - 2 * lag, 0, n_tiles - 1)

    def const(shape):
        return pl.BlockSpec(shape, lambda s: (0,) * len(shape))

    def per_seq(shape, tile):
        return pl.BlockSpec(shape, lambda s: (tile(s) // nt, 0, 0))

    def per_tile(shape, tile):
        return pl.BlockSpec(shape, lambda s: (tile(s) // nt, tile(s) % nt, 0))

    in_specs = [
        per_tile((nbt, lt, D_MODEL), tile_a),
        per_tile((nbt, lt, D_MODEL), tile_c),
        per_seq((nbt, N_MOD, D_MODEL), tile_a),
        per_seq((nbt, N_MOD, D_MODEL), tile_c),
        per_seq((nbt, CONV_HDR, CONV_DIM), tile_b),
        per_seq((nbt, SSD_WIDTH, D_STATE), tile_b),
        const((4, D_MODEL)),
        const((D_MODEL, PROJ_W)),
        const((CONV_W, CONV_DIM)),
        const((1, CONV_DIM)),
        const((1, LANES)),
        const((1, LANES)),
        const((1, SSD_WIDTH)),
        const((1, SSD_WIDTH)),
        const((1, CMLP_WIDTH)),
        const((1, CMLP_WIDTH)),
        const((CMLP_GROUPS, ch, ch)),
        const((ch, CMLP_WIDTH)),
        const((2 * LANES, 2 * SSD_WIDTH)),
        const((D_MODEL, D_MODEL)),
    ]
    out_specs = [
        per_tile((nbt, lt, D_MODEL), tile_c),
        per_seq((nbt, CONV_HDR, CONV_DIM), tile_b),
        per_seq((nbt, SSD_WIDTH, D_STATE), tile_b),
    ]
    out_shape = [
        jax.ShapeDtypeStruct((nb, seq, D_MODEL), F32),
        jax.ShapeDtypeStruct((nb, CONV_HDR, CONV_DIM), F32),
        jax.ShapeDtypeStruct((nb, SSD_WIDTH, D_STATE), F32),
    ]
    if emit_v:
        out_specs.append(per_tile((nbt, lt, CMLP_WIDTH), tile_b))
        out_shape.append(jax.ShapeDtypeStruct((nb, seq, CMLP_WIDTH), F32))
    slot = [
        pltpu.VMEM((nbt, lt + CONV_HDR, CONV_DIM), F32),
        pltpu.VMEM((tm, SSD_WIDTH), F32),
        pltpu.VMEM((tm, LANES), F32),
        pltpu.VMEM((tm, CMLP_WIDTH), F32),
        pltpu.VMEM((tm, CMLP_WIDTH), F32),
        pltpu.VMEM((tm, D_MODEL), BF16),
    ]
    assert len(slot) == N_SLOT_BUFS
    scratch = slot * (2 if pipelined else 1) + [
        pltpu.VMEM((SSD_GROUPS, D_STATE, GROUP_W), F32),
    ]
    return pl.pallas_call(
        functools.partial(_mixer_kernel, nbt=nbt, lt=lt, ch=ch, nt=nt, n_tiles=n_tiles, pipelined=pipelined,
                          emit_v=emit_v),
        grid=(n_steps,),
        in_specs=in_specs,
        out_specs=out_specs,
        out_shape=out_shape,
        scratch_shapes=scratch,
        compiler_params=pltpu.CompilerParams(
            dimension_semantics=("arbitrary",), vmem_limit_bytes=VMEM_LIMIT),
        name="mixer",
    )(x, x, mod, mod, cinit, sinit, ng, win, cw, cb, dtb, alog, dskip, sng, vg, vb, ws, bs, e2, wout)


FF_CHUNK = 1024


def _ffn_kernel(x_ref, mod_ref, ng_ref, w1_ref, w2_ref, o_ref, *, nbt, lt):
    tm = nbt * lt

    def rows(m):
        if nbt == 1:
            return m
        return jnp.broadcast_to(m[:, None, :], (nbt, lt, m.shape[-1])).reshape(tm, m.shape[-1])

    x = x_ref[...].reshape(tm, D_MODEL)
    mod = mod_ref[...]
    sh2, sc2, g2 = mod[:, 3, :], mod[:, 4, :], mod[:, 5, :]
    h = _rms(x) * rows(ng_ref[2:3, :] * (1.0 + sc2)) + rows(sh2)
    hb = h.astype(BF16)
    f = jnp.zeros((tm, D_MODEL), F32)
    for k in range(D_FF // FF_CHUNK):
        a = jnp.maximum(_dot(hb, w1_ref[:, k * FF_CHUNK:(k + 1) * FF_CHUNK]), 0.0)
        f = f + _dot((a * a).astype(BF16), w2_ref[k * FF_CHUNK:(k + 1) * FF_CHUNK, :])
    o = x + rows(g2) * (_rms(f) * ng_ref[3:4, :])
    o_ref[...] = o.reshape(nbt, lt, D_MODEL)


def _ffn_call(x, mod, ng, w1, w2, *, nbt, lt):
    nb, seq, _ = x.shape
    return pl.pallas_call(
        functools.partial(_ffn_kernel, nbt=nbt, lt=lt),
        grid=(nb // nbt, seq // lt),
        in_specs=[
            pl.BlockSpec((nbt, lt, D_MODEL), lambda bi, ti: (bi, ti, 0)),
            pl.BlockSpec((nbt, N_MOD, D_MODEL), lambda bi, ti: (bi, 0, 0)),
            pl.BlockSpec((4, D_MODEL), lambda bi, ti: (0, 0)),
            pl.BlockSpec((D_MODEL, D_FF), lambda bi, ti: (0, 0)),
            pl.BlockSpec((D_FF, D_MODEL), lambda bi, ti: (0, 0)),
        ],
        out_specs=pl.BlockSpec((nbt, lt, D_MODEL), lambda bi, ti: (bi, ti, 0)),
        out_shape=jax.ShapeDtypeStruct((nb, seq, D_MODEL), F32),
        compiler_params=pltpu.CompilerParams(
            dimension_semantics=("arbitrary", "arbitrary"), vmem_limit_bytes=VMEM_LIMIT),
        name="ffn",
    )(x, mod, ng, w1, w2)


def _expand_matrix():
    e = np.zeros((2 * LANES, 2 * SSD_WIDTH), np.float32)
    for blk in range(2):
        for l in range(2 * SSD_HEADS):
            hd = l % SSD_HEADS
            e[blk * LANES + l, blk * SSD_WIDTH + hd * SSD_HEAD_DIM:blk * SSD_WIDTH + (hd + 1) * SSD_HEAD_DIM] = 1.0
    return jnp.asarray(e, BF16)


def _tile_plan(nb, seq):
    if seq >= 512:
        return 1, 512, CMLP_CHUNK
    assert seq <= CMLP_CHUNK and seq % 16 == 0 and (nb * seq) % 8 == 0
    return nb, seq, seq


def _layer_params(l, w_in, w_out, w_ff1, w_ff2, conv_w, conv_b, dt_bias, a_log, d_skip,
                  ssd_norm_g, v_ln_g, v_ln_b):
    wi = w_in[l]
    z0, x0, d0, u0 = 0, SSD_WIDTH, SSD_WIDTH + CONV_DIM, SSD_WIDTH + CONV_DIM + SSD_HEADS
    reps = LANES // SSD_HEADS
    win = jnp.concatenate([
        wi[:, z0:x0], wi[:, x0:d0], wi[:, u0:u0 + CMLP_WIDTH], wi[:, u0 + CMLP_WIDTH:],
        jnp.tile(wi[:, d0:u0], (1, reps))], axis=1).astype(BF16)
    return dict(
        win=win, wout=w_out[l].astype(BF16), w1=w_ff1[l].astype(BF16), w2=w_ff2[l].astype(BF16),
        cw=conv_w[l], cb=conv_b[l].reshape(1, CONV_DIM),
        dtb=jnp.tile(dt_bias[l], reps).reshape(1, LANES),
        alog=jnp.tile(a_log[l], reps).reshape(1, LANES),
        dskip=jnp.repeat(d_skip[l], SSD_HEAD_DIM).reshape(1, SSD_WIDTH),
        sng=ssd_norm_g[l].reshape(1, SSD_WIDTH),
        vg=v_ln_g[l].reshape(1, CMLP_WIDTH), vb=v_ln_b[l].reshape(1, CMLP_WIDTH))


def _block(x, mod, cinit, sinit, p, ng, ws_l, bs_l, e2, emit_v):
    nb, seq, _ = x.shape
    nbt, lt, ch = _tile_plan(nb, seq)
    ws = ws_l[:, :ch, :ch]
    bs = jnp.repeat(bs_l[:, :ch].T, CMLP_GDIM, axis=1)
    cin8 = jnp.pad(cinit, ((0, 0), (CONV_HDR - (CONV_W - 1), 0), (0, 0)))
    outs = _mixer_call(x, mod, cin8, sinit.reshape(nb, SSD_WIDTH, D_STATE), ng, p["win"], p["cw"], p["cb"],
                       p["dtb"], p["alog"], p["dskip"], p["sng"], p["vg"], p["vb"], ws, bs, e2, p["wout"],
                       nbt=nbt, lt=lt, ch=ch, emit_v=emit_v)
    x1, cout, sout = outs[:3]
    x2 = _ffn_call(x1, mod, ng, p["w1"], p["w2"], nbt=nbt, lt=lt)
    conv_state = cout[:, CONV_HDR - (CONV_W - 1):, :]
    ssd_state = sout.reshape(nb, SSD_HEADS, SSD_HEAD_DIM, D_STATE)
    return x2, conv_state, ssd_state, (outs[3] if emit_v else None)


def kernel(x_prompt, x_sample, state_conv, state_ssd, c_prompt, c_sample, w_mod, b_mod, norm_g, w_in, conv_w,
           conv_b, dt_bias, a_log, d_skip, ssd_norm_g, v_ln_g, v_ln_b, w_s, b_s, w_out, w_ff1, w_ff2):
    bp = x_prompt.shape[0]
    bs_ = x_sample.shape[0]
    e2 = _expand_matrix()
    mod = _mod_call(jnp.concatenate([c_prompt, c_sample], axis=0), w_mod, b_mod)
    mod = mod.reshape(DEPTH, bp + bs_, N_MOD, D_MODEL)
    conv_p0 = jnp.zeros((bp, CONV_W - 1, CONV_DIM), F32)
    ssd_p0 = jnp.zeros((bp, SSD_HEADS, SSD_HEAD_DIM, D_STATE), F32)
    xp, xs = x_prompt, x_sample
    conv_p, ssd_p, conv_s, ssd_s, v_s = [], [], [], [], []
    for l in range(DEPTH):
        p = _layer_params(l, w_in, w_out, w_ff1, w_ff2, conv_w, conv_b, dt_bias, a_log, d_skip,
                          ssd_norm_g, v_ln_g, v_ln_b)
        xp, cp, hp, _ = _block(xp, mod[l, :bp], conv_p0, ssd_p0, p, norm_g[l], w_s[l], b_s[l], e2, False)
        xs, cs, hs, vs = _block(xs, mod[l, bp:], state_conv[l], state_ssd[l], p, norm_g[l], w_s[l], b_s[l], e2, True)
        conv_p.append(cp)
        ssd_p.append(hp)
        conv_s.append(cs)
        ssd_s.append(hs)
        v_s.append(vs)
    return (xp, xs, jnp.stack(conv_p), jnp.stack(ssd_p), jnp.stack(conv_s), jnp.stack(ssd_s), jnp.stack(v_s))
```

```python
import functools

import numpy as np
import jax
import jax.numpy as jnp
from jax import lax
from jax.experimental import pallas as pl
from jax.experimental.pallas import tpu as pltpu

F32 = jnp.float32
BF16 = jnp.bfloat16

D_MODEL = 1024
DEPTH = 4
SSD_WIDTH = 512
SSD_HEAD_DIM = 64
SSD_HEADS = 8
SSD_GROUPS = 2
SSD_HPG = 4
D_STATE = 128
CONV_W = 4
CONV_DIM = 1024
CMLP_WIDTH = 512
CMLP_GROUPS = 4
CMLP_GDIM = 128
CMLP_CHUNK = 128
D_FF = 4096
N_MOD = 6
RMS_EPS = 1e-6
LN_EPS = 1e-5

LANES = 128
CONV_HDR = 8
N_SLABS = CONV_DIM // LANES
GROUP_W = SSD_HPG * SSD_HEAD_DIM
OFF_Z, OFF_XBC, OFF_U, OFF_V, OFF_DT = 0, 512, 1536, 2048, 2560
PROJ_W = OFF_DT + LANES
VMEM_LIMIT = 56 * 1024 * 1024


def _silu(x):
    h = 0.5 * x
    return h + h * jnp.tanh(h)


GELU_C0 = float(np.sqrt(2.0 / np.pi))
GELU_C1 = GELU_C0 * 0.044715


def _gelu_tanh(x):
    h = 0.5 * x
    return h + h * jnp.tanh(x * (GELU_C0 + GELU_C1 * (x * x)))


def _rms(x, eps=RMS_EPS):
    return x * lax.rsqrt(jnp.mean(x * x, axis=-1, keepdims=True) + eps)


def _dot(a, b):
    return jnp.dot(a, b, preferred_element_type=F32)


def _mod_kernel(c_ref, w_ref, b_ref, o_ref):
    s = _silu(c_ref[...]).astype(BF16)
    o_ref[0] = _dot(s, w_ref[0].astype(BF16)) + b_ref[0]


def _mod_call(c_all, w_mod, b_mod):
    nb = c_all.shape[0]
    tn = 1536
    n = N_MOD * D_MODEL
    return pl.pallas_call(
        _mod_kernel,
        grid=(DEPTH, n // tn),
        in_specs=[
            pl.BlockSpec((nb, D_MODEL), lambda l, j: (0, 0)),
            pl.BlockSpec((1, D_MODEL, tn), lambda l, j: (l, 0, j)),
            pl.BlockSpec((1, 1, tn), lambda l, j: (l, 0, j)),
        ],
        out_specs=pl.BlockSpec((1, nb, tn), lambda l, j: (l, 0, j)),
        out_shape=jax.ShapeDtypeStruct((DEPTH, nb, n), F32),
        compiler_params=pltpu.CompilerParams(
            dimension_semantics=("arbitrary", "arbitrary"), vmem_limit_bytes=VMEM_LIMIT),
        name="mod",
    )(c_all, w_mod, b_mod.reshape(DEPTH, 1, n))


N_SLOT_BUFS = 6


def _mixer_kernel(xa_ref, xc_ref, moda_ref, modc_ref, cinit_ref, sinit_ref, ng_ref, win_ref, cw_ref, cb_ref,
                  dtb_ref, alog_ref, dskip_ref, sng_ref, vg_ref, vb_ref, ws_ref, bs_ref,
                  wout_ref,
                  *refs, nbt, lt, ch, nt, n_tiles, pipelined, emit_v):
    n_out = 4 if emit_v else 3
    x1_ref, cout_ref, sout_ref = refs[:3]
    vout_ref = refs[3] if emit_v else None
    scratch = refs[n_out:]
    slots = [scratch[k * N_SLOT_BUFS:(k + 1) * N_SLOT_BUFS] for k in range(2 if pipelined else 1)]
    s_ref, *xa_refs = scratch[len(slots) * N_SLOT_BUFS:]
    stack_k = ch == LANES
    strided_conv = len(xa_refs) > 0

    s_id = pl.program_id(0)
    tm = nbt * lt
    nc = lt // ch

    def rows(m):
        if nbt == 1:
            return m
        return jnp.broadcast_to(m[:, None, :], (nbt, lt, m.shape[-1])).reshape(tm, m.shape[-1])

    n_sub = ch // 8
    sub_i = lax.broadcasted_iota(jnp.int32, (n_sub, 8, LANES), 1)
    lane_i = lax.broadcasted_iota(jnp.int32, (ch, LANES), 1)
    ii = lax.broadcasted_iota(jnp.int32, (ch, ch), 0)
    jj = lax.broadcasted_iota(jnp.int32, (ch, ch), 1)
    causal = ii >= jj
    head_of_lane = lax.broadcasted_iota(jnp.int32, (ch, GROUP_W), 1) // SSD_HEAD_DIM

    def expand_heads(v):
        cols = [jnp.broadcast_to(v[:, h:h + 1], (ch, LANES)) for h in range(SSD_HEADS)]
        return jnp.concatenate([jnp.where(lane_i < SSD_HEAD_DIM, cols[2 * q], cols[2 * q + 1])
                                for q in range(SSD_HEADS // 2)], axis=1)

    def transpose_rows(v):
        if ch == LANES:
            return v.T
        return jnp.concatenate([v, jnp.zeros((LANES - ch, LANES), F32)], axis=0).T[:, :ch]

    def cumsum_rows(a):
        a3 = a.reshape(n_sub, 8, LANES)
        for sft in (1, 2, 4):
            a3 = a3 + jnp.where(sub_i >= sft, pltpu.roll(a3, sft, 1), 0.0)
        carry = jnp.zeros((1, LANES), F32)
        out = []
        for k in range(n_sub):
            out.append(a3[k] + carry)
            carry = carry + a3[k, 7:8, :]
        return jnp.concatenate(out, axis=0)

    def project(slot):
        xbc_ref, z_ref, dt_ref, u_ref, v_ref, _ = slot
        x = xa_ref[...].reshape(tm, D_MODEL)
        mod = moda_ref[...]
        sh1, sc1 = mod[:, 0, :], mod[:, 1, :]
        h = _rms(x) * rows(ng_ref[0:1, :] * (1.0 + sc1)) + rows(sh1)
        hb = h.astype(BF16)
        yield
        z_ref[...] = _silu(_dot(hb, win_ref[:, OFF_Z:OFF_XBC]))
        yield
        xbc = _dot(hb, win_ref[:, OFF_XBC:OFF_U])
        for b in range(nbt):
            for j in range(N_SLABS):
                xbc_ref[j, b, CONV_HDR:CONV_HDR + lt, :] = xbc[b * lt:(b + 1) * lt, j * LANES:(j + 1) * LANES]
        yield
        u_ref[...] = _gelu_tanh(_dot(hb, win_ref[:, OFF_U:OFF_V]))
        yield
        v_ref[...] = _gelu_tanh(_dot(hb, win_ref[:, OFF_V:OFF_DT]))
        yield
        dt_ref[...] = _dot(hb, win_ref[:, OFF_DT:PROJ_W])

    def init_tile(slot):
        xbc_ref = slot[0]
        for b in range(nbt):
            for j in range(N_SLABS):
                xbc_ref[j, b, 0:CONV_HDR, :] = cinit_ref[b, :, j * LANES:(j + 1) * LANES]
        if nbt == 1:
            for g in range(SSD_GROUPS):
                s_ref[g] = sinit_ref[0, g * GROUP_W:(g + 1) * GROUP_W, :].T

    def finish_tile(slot):
        xbc_ref = slot[0]
        for b in range(nbt):
            for j in range(N_SLABS):
                cout_ref[b, :, j * LANES:(j + 1) * LANES] = xbc_ref[j, b, lt:lt + CONV_HDR, :]
        if nbt == 1:
            for g in range(SSD_GROUPS):
                sout_ref[0, g * GROUP_W:(g + 1) * GROUP_W, :] = s_ref[g].T

    def mix(slot, next_slot):
        xbc_ref, z_ref, dt_ref, u_ref, v_ref, y_ref = slot
        a_row = -jnp.exp(alog_ref[...])
        w_mix = [jnp.where(causal, ws_ref[g], 0.0).astype(BF16) for g in range(CMLP_GROUPS)]
        if stack_k:
            w_mix = [jnp.concatenate(w_mix[2 * q:2 * q + 2], axis=1) for q in range(CMLP_GROUPS // 2)]
        zeros_gd = jnp.zeros((ch, CMLP_GDIM), BF16)
        for i in range(nbt * nc):
            b, c = divmod(i, nc)
            r0 = c * ch
            rs = slice(i * ch, (i + 1) * ch)

            if strided_conv:
                xa_ref = xa_refs[i]
                first = CONV_HDR - (CONV_W - 1)
                for j in range(N_SLABS):
                    col = slice(j * LANES, (j + 1) * LANES)
                    phase = {m: xbc_ref[j, b, pl.ds(r0 + m, n_sub, stride=8), :]
                             for m in range(first, first + 8 + CONV_W - 1)}
                    for p in range(8):
                        acc = cb_ref[:, col]
                        for k in range(CONV_W):
                            acc = acc + cw_ref[k:k + 1, col] * phase[first + p + k]
                        xa_ref[j, pl.ds(p, n_sub, stride=8), :] = _silu(acc)
                xbc_a = jnp.concatenate([xa_ref[j] for j in range(N_SLABS)], axis=1)
            else:
                big = jnp.concatenate([xbc_ref[j, b, r0:r0 + ch + CONV_HDR, :] for j in range(N_SLABS)], axis=1)
                acc = cb_ref[...] + cw_ref[CONV_W - 1:CONV_W, :] * big[CONV_HDR:, :]
                for k in range(CONV_W - 1):
                    acc = acc + cw_ref[k:k + 1, :] * pltpu.roll(big, CONV_W - 1 - k, 0)[CONV_HDR:, :]
                xbc_a = _silu(acc)
            xs = xbc_a[:, :SSD_WIDTH]
            xsb = xs.astype(BF16)
            bmb = xbc_a[:, SSD_WIDTH:SSD_WIDTH + SSD_GROUPS * D_STATE].astype(BF16)
            cmb = xbc_a[:, SSD_WIDTH + SSD_GROUPS * D_STATE:].astype(BF16)

            dtr = dt_ref[rs, :] + dtb_ref[...]
            dt = jnp.maximum(dtr, 0.0) + jnp.log1p(jnp.exp(-jnp.abs(dtr)))
            acum = cumsum_rows(dt * a_row)
            acum_e = expand_heads(acum)
            ea_e = jnp.exp(acum_e)
            xw = (xs * (jnp.exp(acum_e[ch - 1:ch, :] - acum_e) * expand_heads(dt))).astype(BF16)
            acum_t = transpose_rows(acum)
            dt_t = transpose_rows(dt)

            if nbt > 1:
                for g in range(SSD_GROUPS):
                    s_ref[g] = sinit_ref[b, g * GROUP_W:(g + 1) * GROUP_W, :].T

            ys = []
            for g in range(SSD_GROUPS):
                gs = slice(g * GROUP_W, (g + 1) * GROUP_W)
                cg = cmb[:, g * D_STATE:(g + 1) * D_STATE]
                bg = bmb[:, g * D_STATE:(g + 1) * D_STATE]
                cbm = lax.dot_general(cg, bg, (((1,), (1,)), ((), ())), preferred_element_type=F32)
                mms, rhss = [], []
                for r in range(SSD_HPG):
                    hh = g * SSD_HPG + r
                    dec = jnp.where(causal, jnp.exp(acum[:, hh:hh + 1] - acum_t[hh:hh + 1, :]), 0.0)
                    mms.append((cbm * (dec * dt_t[hh:hh + 1, :])).astype(BF16))
                    rhss.append(jnp.where(head_of_lane == r, xsb[:, gs], jnp.zeros((ch, GROUP_W), BF16)))
                if stack_k:
                    mms = [jnp.concatenate(mms[2 * q:2 * q + 2], axis=1) for q in range(SSD_HPG // 2)]
                    rhss = [jnp.concatenate(rhss[2 * q:2 * q + 2], axis=0) for q in range(SSD_HPG // 2)]
                yg = None
                for mm, rhs in zip(mms, rhss):
                    part = _dot(mm, rhs)
                    yg = part if yg is None else yg + part
                sg = s_ref[g]
                yg = yg + _dot(cg, sg.astype(BF16)) * ea_e[:, gs]
                ds = lax.dot_general(bg, xw[:, gs], (((0,), (0,)), ((), ())), preferred_element_type=F32)
                s_ref[g] = ea_e[ch - 1:ch, gs] * sg + ds
                ys.append(yg)
            y = jnp.concatenate(ys, axis=1) + dskip_ref[...] * xs
            y = _rms(y * z_ref[rs, :]) * sng_ref[...]
            y_ref[rs, 0:SSD_WIDTH] = y.astype(BF16)

            if nbt > 1:
                for g in range(SSD_GROUPS):
                    sout_ref[b, g * GROUP_W:(g + 1) * GROUP_W, :] = s_ref[g].T

            u = u_ref[rs, :]
            v = v_ref[rs, :]
            vc = v - jnp.mean(v, axis=-1, keepdims=True)
            vn = vc * lax.rsqrt(jnp.mean(vc * vc, axis=-1, keepdims=True) + LN_EPS) * vg_ref[...] + vb_ref[...]
            if emit_v:
                vout_ref[b, r0:r0 + ch, :] = vn
            vnb = vn.astype(BF16)
            gw = 2 * CMLP_GDIM if stack_k else CMLP_GDIM
            for q, wq in enumerate(w_mix):
                sl = slice(q * gw, (q + 1) * gw)
                if stack_k:
                    va, vb_ = vnb[:, q * gw:q * gw + CMLP_GDIM], vnb[:, q * gw + CMLP_GDIM:(q + 1) * gw]
                    rhs = jnp.concatenate([jnp.concatenate([va, zeros_gd], axis=1),
                                           jnp.concatenate([zeros_gd, vb_], axis=1)], axis=0)
                else:
                    rhs = vnb[:, sl]
                mixed = _dot(wq, rhs) + bs_ref[:, sl]
                y_ref[rs, SSD_WIDTH + q * gw:SSD_WIDTH + (q + 1) * gw] = (u[:, sl] * mixed).astype(BF16)
            yield

        for b in range(nbt):
            for j in range(N_SLABS):
                next_slot[0][j, b, 0:CONV_HDR, :] = xbc_ref[j, b, lt:lt + CONV_HDR, :]

    def emit(slot):
        y_ref = slot[5]
        x = xc_ref[...].reshape(tm, D_MODEL)
        g1 = modc_ref[...][:, 2, :]
        m = _dot(y_ref[...], wout_ref[...])
        yield
        x1 = x + rows(g1 * ng_ref[1:2, :]) * _rms(m)
        x1_ref[...] = x1.reshape(nbt, lt, D_MODEL)

    def run(gens, order):
        for k in order:
            next(gens[k], None)
        for g in gens:
            for _ in g:
                pass

    if not pipelined:
        assert nt == 1
        init_tile(slots[0])
        run([project(slots[0]), mix(slots[0], slots[0]), emit(slots[0])], ())
        finish_tile(slots[0])
        return

    assert nbt == 1
    tb = jnp.clip(s_id - 1, 0, n_tiles - 1) % nt
    mix_is_real = jnp.logical_and(s_id >= 1, s_id <= n_tiles)

    @pl.when(s_id == 0)
    def _():
        for ref in slots[1]:
            ref[...] = jnp.zeros(ref.shape, ref.dtype)
        slots[0][5][...] = jnp.zeros(slots[0][5].shape, BF16)

    for par in range(2):
        @pl.when(s_id % 2 == par)
        def _(par=par):
            mine, other = slots[1 - par], slots[par]
            pl.when(tb == 0)(lambda: init_tile(mine))
            run([project(other), mix(mine, other), emit(other)], (2, 0, 1, 0, 0, 2, 1, 0, 1, 0, 1, 0))
            pl.when(jnp.logical_and(tb == nt - 1, mix_is_real))(lambda: finish_tile(mine))


def _mixer_call(x, mod, cinit, sinit, ng, win, cw, cb, dtb, alog, dskip, sng, vg, vb, ws, bs,
                wout, *, nbt, lt, ch, emit_v):
    nb, seq, _ = x.shape
    tm = nbt * lt
    nt = seq // lt
    n_tiles = (nb // nbt) * nt
    pipelined = n_tiles > 1
    lag = 1 if pipelined else 0
    n_steps = n_tiles + 2 * lag

    def tile_a(s):
        return jnp.clip(s, 0, n_tiles - 1)

    def tile_b(s):
        return jnp.clip(s - lag, 0, n_tiles - 1)

    def tile_c(s):
        return jnp.clip(s - 2 * lag, 0, n_tiles - 1)

    def const(shape):
        return pl.BlockSpec(shape, lambda s: (0,) * len(shape))

    def per_seq(shape, tile):
        return pl.BlockSpec(shape, lambda s: (tile(s) // nt, 0, 0))

    def per_tile(shape, tile):
        return pl.BlockSpec(shape, lambda s: (tile(s) // nt, tile(s) % nt, 0))

    in_specs = [
        per_tile((nbt, lt, D_MODEL), tile_a),
        per_tile((nbt, lt, D_MODEL), tile_c),
        per_seq((nbt, N_MOD, D_MODEL), tile_a),
        per_seq((nbt, N_MOD, D_MODEL), tile_c),
        per_seq((nbt, CONV_HDR, CONV_DIM), tile_b),
        per_seq((nbt, SSD_WIDTH, D_STATE), tile_b),
        const((4, D_MODEL)),
        const((D_MODEL, PROJ_W)),
        const((CONV_W, CONV_DIM)),
        const((1, CONV_DIM)),
        const((1, LANES)),
        const((1, LANES)),
        const((1, SSD_WIDTH)),
        const((1, SSD_WIDTH)),
        const((1, CMLP_WIDTH)),
        const((1, CMLP_WIDTH)),
        const((CMLP_GROUPS, ch, ch)),
        const((ch, CMLP_WIDTH)),
        const((D_MODEL, D_MODEL)),
    ]
    out_specs = [
        per_tile((nbt, lt, D_MODEL), tile_c),
        per_seq((nbt, CONV_HDR, CONV_DIM), tile_b),
        per_seq((nbt, SSD_WIDTH, D_STATE), tile_b),
    ]
    out_shape = [
        jax.ShapeDtypeStruct((nb, seq, D_MODEL), F32),
        jax.ShapeDtypeStruct((nb, CONV_HDR, CONV_DIM), F32),
        jax.ShapeDtypeStruct((nb, SSD_WIDTH, D_STATE), F32),
    ]
    if emit_v:
        out_specs.append(per_tile((nbt, lt, CMLP_WIDTH), tile_b))
        out_shape.append(jax.ShapeDtypeStruct((nb, seq, CMLP_WIDTH), F32))
    slot = [
        pltpu.VMEM((N_SLABS, nbt, lt + CONV_HDR, LANES), F32),
        pltpu.VMEM((tm, SSD_WIDTH), F32),
        pltpu.VMEM((tm, LANES), F32),
        pltpu.VMEM((tm, CMLP_WIDTH), F32),
        pltpu.VMEM((tm, CMLP_WIDTH), F32),
        pltpu.VMEM((tm, D_MODEL), BF16),
    ]
    assert len(slot) == N_SLOT_BUFS
    scratch = slot * (2 if pipelined else 1) + [
        pltpu.VMEM((SSD_GROUPS, D_STATE, GROUP_W), F32),
    ]
    if ch == LANES:
        scratch += [pltpu.VMEM((N_SLABS, ch, LANES), F32)] * (tm // ch)
    return pl.pallas_call(
        functools.partial(_mixer_kernel, nbt=nbt, lt=lt, ch=ch, nt=nt, n_tiles=n_tiles, pipelined=pipelined,
                          emit_v=emit_v),
        grid=(n_steps,),
        in_specs=in_specs,
        out_specs=out_specs,
        out_shape=out_shape,
        scratch_shapes=scratch,
        compiler_params=pltpu.CompilerParams(
            dimension_semantics=("arbitrary",), vmem_limit_bytes=VMEM_LIMIT),
        name="mixer",
    )(x, x, mod, mod, cinit, sinit, ng, win, cw, cb, dtb, alog, dskip, sng, vg, vb, ws, bs, wout)


FF_CHUNK = 1024


def _ffn_kernel(x_ref, mod_ref, ng_ref, w1_ref, w2_ref, o_ref, *, nbt, lt):
    tm = nbt * lt

    def rows(m):
        if nbt == 1:
            return m
        return jnp.broadcast_to(m[:, None, :], (nbt, lt, m.shape[-1])).reshape(tm, m.shape[-1])

    x = x_ref[...].reshape(tm, D_MODEL)
    mod = mod_ref[...]
    sh2, sc2, g2 = mod[:, 3, :], mod[:, 4, :], mod[:, 5, :]
    h = _rms(x) * rows(ng_ref[2:3, :] * (1.0 + sc2)) + rows(sh2)
    hb = h.astype(BF16)
    f = jnp.zeros((tm, D_MODEL), F32)
    for k in range(D_FF // FF_CHUNK):
        a = jnp.maximum(_dot(hb, w1_ref[:, k * FF_CHUNK:(k + 1) * FF_CHUNK]), 0.0)
        f = f + _dot((a * a).astype(BF16), w2_ref[k * FF_CHUNK:(k + 1) * FF_CHUNK, :])
    o = x + rows(g2) * (_rms(f) * ng_ref[3:4, :])
    o_ref[...] = o.reshape(nbt, lt, D_MODEL)


def _ffn_call(x, mod, ng, w1, w2, *, nbt, lt):
    nb, seq, _ = x.shape
    return pl.pallas_call(
        functools.partial(_ffn_kernel, nbt=nbt, lt=lt),
        grid=(nb // nbt, seq // lt),
        in_specs=[
            pl.BlockSpec((nbt, lt, D_MODEL), lambda bi, ti: (bi, ti, 0)),
            pl.BlockSpec((nbt, N_MOD, D_MODEL), lambda bi, ti: (bi, 0, 0)),
            pl.BlockSpec((4, D_MODEL), lambda bi, ti: (0, 0)),
            pl.BlockSpec((D_MODEL, D_FF), lambda bi, ti: (0, 0)),
            pl.BlockSpec((D_FF, D_MODEL), lambda bi, ti: (0, 0)),
        ],
        out_specs=pl.BlockSpec((nbt, lt, D_MODEL), lambda bi, ti: (bi, ti, 0)),
        out_shape=jax.ShapeDtypeStruct((nb, seq, D_MODEL), F32),
        compiler_params=pltpu.CompilerParams(
            dimension_semantics=("arbitrary", "arbitrary"), vmem_limit_bytes=VMEM_LIMIT),
        name="ffn",
    )(x, mod, ng, w1, w2)


def _tile_plan(nb, seq):
    if seq >= 512:
        return 1, 512, CMLP_CHUNK
    assert seq <= CMLP_CHUNK and seq % 16 == 0 and (nb * seq) % 8 == 0
    return nb, seq, seq


def _layer_params(l, w_in, w_out, w_ff1, w_ff2, conv_w, conv_b, dt_bias, a_log, d_skip,
                  ssd_norm_g, v_ln_g, v_ln_b):
    wi = w_in[l]
    z0, x0, d0, u0 = 0, SSD_WIDTH, SSD_WIDTH + CONV_DIM, SSD_WIDTH + CONV_DIM + SSD_HEADS
    reps = LANES // SSD_HEADS
    win = jnp.concatenate([
        wi[:, z0:x0], wi[:, x0:d0], wi[:, u0:u0 + CMLP_WIDTH], wi[:, u0 + CMLP_WIDTH:],
        jnp.tile(wi[:, d0:u0], (1, reps))], axis=1).astype(BF16)
    return dict(
        win=win, wout=w_out[l].astype(BF16), w1=w_ff1[l].astype(BF16), w2=w_ff2[l].astype(BF16),
        cw=conv_w[l], cb=conv_b[l].reshape(1, CONV_DIM),
        dtb=jnp.tile(dt_bias[l], reps).reshape(1, LANES),
        alog=jnp.tile(a_log[l], reps).reshape(1, LANES),
        dskip=jnp.repeat(d_skip[l], SSD_HEAD_DIM).reshape(1, SSD_WIDTH),
        sng=ssd_norm_g[l].reshape(1, SSD_WIDTH),
        vg=v_ln_g[l].reshape(1, CMLP_WIDTH), vb=v_ln_b[l].reshape(1, CMLP_WIDTH))


def _block(x, mod, cinit, sinit, p, ng, ws_l, bs_l, emit_v):
    nb, seq, _ = x.shape
    nbt, lt, ch = _tile_plan(nb, seq)
    ws = ws_l[:, :ch, :ch]
    bs = jnp.repeat(bs_l[:, :ch].T, CMLP_GDIM, axis=1)
    cin8 = jnp.pad(cinit, ((0, 0), (CONV_HDR - (CONV_W - 1), 0), (0, 0)))
    outs = _mixer_call(x, mod, cin8, sinit.reshape(nb, SSD_WIDTH, D_STATE), ng, p["win"], p["cw"], p["cb"],
                       p["dtb"], p["alog"], p["dskip"], p["sng"], p["vg"], p["vb"], ws, bs, p["wout"],
                       nbt=nbt, lt=lt, ch=ch, emit_v=emit_v)
    x1, cout, sout = outs[:3]
    x2 = _ffn_call(x1, mod, ng, p["w1"], p["w2"], nbt=nbt, lt=lt)
    conv_state = cout[:, CONV_HDR - (CONV_W - 1):, :]
    ssd_state = sout.reshape(nb, SSD_HEADS, SSD_HEAD_DIM, D_STATE)
    return x2, conv_state, ssd_state, (outs[3] if emit_v else None)


def kernel(x_prompt, x_sample, state_conv, state_ssd, c_prompt, c_sample, w_mod, b_mod, norm_g, w_in, conv_w,
           conv_b, dt_bias, a_log, d_skip, ssd_norm_g, v_ln_g, v_ln_b, w_s, b_s, w_out, w_ff1, w_ff2):
    bp = x_prompt.shape[0]
    bs_ = x_sample.shape[0]
    mod = _mod_call(jnp.concatenate([c_prompt, c_sample], axis=0), w_mod, b_mod)
    mod = mod.reshape(DEPTH, bp + bs_, N_MOD, D_MODEL)
    conv_p0 = jnp.zeros((bp, CONV_W - 1, CONV_DIM), F32)
    ssd_p0 = jnp.zeros((bp, SSD_HEADS, SSD_HEAD_DIM, D_STATE), F32)
    xp, xs = x_prompt, x_sample
    conv_p, ssd_p, conv_s, ssd_s, v_s = [], [], [], [], []
    for l in range(DEPTH):
        p = _layer_params(l, w_in, w_out, w_ff1, w_ff2, conv_w, conv_b, dt_bias, a_log, d_skip,
                          ssd_norm_g, v_ln_g, v_ln_b)
        xp, cp, hp, _ = _block(xp, mod[l, :bp], conv_p0, ssd_p0, p, norm_g[l], w_s[l], b_s[l], False)
        xs, cs, hs, vs = _block(xs, mod[l, bp:], state_conv[l], state_ssd[l], p, norm_g[l], w_s[l], b_s[l], True)
        conv_p.append(cp)
        ssd_p.append(hp)
        conv_s.append(cs)
        ssd_s.append(hs)
        v_s.append(vs)
    return (xp, xs, jnp.stack(conv_p), jnp.stack(ssd_p), jnp.stack(conv_s), jnp.stack(ssd_s), jnp.stack(v_s))
```

```python
import functools

import numpy as np
import jax
import jax.numpy as jnp
from jax import lax
from jax.experimental import pallas as pl
from jax.experimental.pallas import tpu as pltpu

F32 = jnp.float32
BF16 = jnp.bfloat16

D_MODEL = 1024
DEPTH = 4
SSD_WIDTH = 512
SSD_HEAD_DIM = 64
SSD_HEADS = 8
SSD_GROUPS = 2
SSD_HPG = 4
D_STATE = 128
CONV_W = 4
CONV_DIM = 1024
CMLP_WIDTH = 512
CMLP_GROUPS = 4
CMLP_GDIM = 128
CMLP_CHUNK = 128
D_FF = 4096
N_MOD = 6
RMS_EPS = 1e-6
LN_EPS = 1e-5

LANES = 128
CONV_HDR = 8
N_SLABS = CONV_DIM // LANES
GROUP_W = SSD_HPG * SSD_HEAD_DIM
OFF_Z, OFF_XBC, OFF_U, OFF_V, OFF_DT = 0, 512, 1536, 2048, 2560
PROJ_W = OFF_DT + LANES
VMEM_LIMIT = 56 * 1024 * 1024


def _silu(x):
    h = 0.5 * x
    return h + h * jnp.tanh(h)


GELU_C0 = float(np.sqrt(2.0 / np.pi))
GELU_C1 = GELU_C0 * 0.044715


def _gelu_tanh(x):
    h = 0.5 * x
    return h + h * jnp.tanh(x * (GELU_C0 + GELU_C1 * (x * x)))


def _rms(x, eps=RMS_EPS):
    return x * lax.rsqrt(jnp.mean(x * x, axis=-1, keepdims=True) + eps)


def _dot(a, b):
    return jnp.dot(a, b, preferred_element_type=F32)


def _mod_kernel(c_ref, w_ref, b_ref, o_ref):
    s = _silu(c_ref[...]).astype(BF16)
    o_ref[0] = _dot(s, w_ref[0].astype(BF16)) + b_ref[0]


def _mod_call(c_all, w_mod, b_mod):
    nb = c_all.shape[0]
    tn = 1536
    n = N_MOD * D_MODEL
    return pl.pallas_call(
        _mod_kernel,
        grid=(DEPTH, n // tn),
        in_specs=[
            pl.BlockSpec((nb, D_MODEL), lambda l, j: (0, 0)),
            pl.BlockSpec((1, D_MODEL, tn), lambda l, j: (l, 0, j)),
            pl.BlockSpec((1, 1, tn), lambda l, j: (l, 0, j)),
        ],
        out_specs=pl.BlockSpec((1, nb, tn), lambda l, j: (l, 0, j)),
        out_shape=jax.ShapeDtypeStruct((DEPTH, nb, n), F32),
        compiler_params=pltpu.CompilerParams(
            dimension_semantics=("arbitrary", "arbitrary"), vmem_limit_bytes=VMEM_LIMIT),
        name="mod",
    )(c_all, w_mod, b_mod.reshape(DEPTH, 1, n))


N_SLOT_BUFS = 6


def _mixer_kernel(xa_ref, xc_ref, moda_ref, modc_ref, cinit_ref, sinit_ref, ng_ref, win_ref, cw_ref, cb_ref,
                  dtb_ref, alog_ref, dskip_ref, sng_ref, vg_ref, vb_ref, ws_ref, bs_ref,
                  wout_ref,
                  *refs, nbt, lt, ch, nt, n_tiles, pipelined, emit_v):
    n_out = 4 if emit_v else 3
    x1_ref, cout_ref, sout_ref = refs[:3]
    vout_ref = refs[3] if emit_v else None
    scratch = refs[n_out:]
    slots = [scratch[k * N_SLOT_BUFS:(k + 1) * N_SLOT_BUFS] for k in range(2 if pipelined else 1)]
    s_ref, *xa_refs = scratch[len(slots) * N_SLOT_BUFS:]
    stack_k = ch == LANES
    strided_conv = len(xa_refs) > 0

    s_id = pl.program_id(0)
    tm = nbt * lt
    nc = lt // ch

    def rows(m):
        if nbt == 1:
            return m
        return jnp.broadcast_to(m[:, None, :], (nbt, lt, m.shape[-1])).reshape(tm, m.shape[-1])

    n_sub = ch // 8
    sub_i = lax.broadcasted_iota(jnp.int32, (n_sub, 8, LANES), 1)
    lane_i = lax.broadcasted_iota(jnp.int32, (ch, LANES), 1)
    ii = lax.broadcasted_iota(jnp.int32, (ch, ch), 0)
    jj = lax.broadcasted_iota(jnp.int32, (ch, ch), 1)
    causal = ii >= jj
    head_of_lane = lax.broadcasted_iota(jnp.int32, (ch, GROUP_W), 1) // SSD_HEAD_DIM

    def expand_heads(v):
        cols = [jnp.broadcast_to(v[:, h:h + 1], (ch, LANES)) for h in range(SSD_HEADS)]
        return jnp.concatenate([jnp.where(lane_i < SSD_HEAD_DIM, cols[2 * q], cols[2 * q + 1])
                                for q in range(SSD_HEADS // 2)], axis=1)

    def transpose_rows(v):
        if ch == LANES:
            return v.T
        return jnp.concatenate([v, jnp.zeros((LANES - ch, LANES), F32)], axis=0).T[:, :ch]

    def cumsum_rows(a):
        a3 = a.reshape(n_sub, 8, LANES)
        for sft in (1, 2, 4):
            a3 = a3 + jnp.where(sub_i >= sft, pltpu.roll(a3, sft, 1), 0.0)
        carry = jnp.zeros((1, LANES), F32)
        out = []
        for k in range(n_sub):
            out.append(a3[k] + carry)
            carry = carry + a3[k, 7:8, :]
        return jnp.concatenate(out, axis=0)

    def project(slot):
        xbc_ref, z_ref, dt_ref, u_ref, v_ref, _ = slot
        x = xa_ref[...].reshape(tm, D_MODEL)
        mod = moda_ref[...]
        sh1, sc1 = mod[:, 0, :], mod[:, 1, :]
        h = _rms(x) * rows(ng_ref[0:1, :] * (1.0 + sc1)) + rows(sh1)
        hb = h.astype(BF16)
        yield
        z_ref[...] = _silu(_dot(hb, win_ref[:, OFF_Z:OFF_XBC]))
        yield
        xbc = _dot(hb, win_ref[:, OFF_XBC:OFF_U])
        for b in range(nbt):
            for j in range(N_SLABS):
                xbc_ref[j, b, CONV_HDR:CONV_HDR + lt, :] = xbc[b * lt:(b + 1) * lt, j * LANES:(j + 1) * LANES]
        yield
        u_ref[...] = _gelu_tanh(_dot(hb, win_ref[:, OFF_U:OFF_V]))
        yield
        vd = _dot(hb, win_ref[:, OFF_V:PROJ_W])
        v_ref[...] = _gelu_tanh(vd[:, :CMLP_WIDTH])
        dt_ref[...] = vd[:, CMLP_WIDTH:]
        yield

    def init_tile(slot):
        xbc_ref = slot[0]
        for b in range(nbt):
            for j in range(N_SLABS):
                xbc_ref[j, b, 0:CONV_HDR, :] = cinit_ref[b, :, j * LANES:(j + 1) * LANES]
        if nbt == 1:
            for g in range(SSD_GROUPS):
                s_ref[g] = sinit_ref[0, g * GROUP_W:(g + 1) * GROUP_W, :].T

    def finish_tile(slot):
        xbc_ref = slot[0]
        for b in range(nbt):
            for j in range(N_SLABS):
                cout_ref[b, :, j * LANES:(j + 1) * LANES] = xbc_ref[j, b, lt:lt + CONV_HDR, :]
        if nbt == 1:
            for g in range(SSD_GROUPS):
                sout_ref[0, g * GROUP_W:(g + 1) * GROUP_W, :] = s_ref[g].T

    def mix(slot, next_slot):
        xbc_ref, z_ref, dt_ref, u_ref, v_ref, y_ref = slot
        a_row = -jnp.exp(alog_ref[...])
        w_mix = [jnp.where(causal, ws_ref[g], 0.0).astype(BF16) for g in range(CMLP_GROUPS)]
        if stack_k:
            w_mix = [jnp.concatenate(w_mix[2 * q:2 * q + 2], axis=1) for q in range(CMLP_GROUPS // 2)]
        zeros_gd = jnp.zeros((ch, CMLP_GDIM), BF16)
        for i in range(nbt * nc):
            b, c = divmod(i, nc)
            r0 = c * ch
            rs = slice(i * ch, (i + 1) * ch)

            if strided_conv:
                xa_ref = xa_refs[i]
                first = CONV_HDR - (CONV_W - 1)
                for j in range(N_SLABS):
                    col = slice(j * LANES, (j + 1) * LANES)
                    phase = {m: xbc_ref[j, b, pl.ds(r0 + m, n_sub, stride=8), :]
                             for m in range(first, first + 8 + CONV_W - 1)}
                    for p in range(8):
                        acc = cb_ref[:, col]
                        for k in range(CONV_W):
                            acc = acc + cw_ref[k:k + 1, col] * phase[first + p + k]
                        xa_ref[j, pl.ds(p, n_sub, stride=8), :] = _silu(acc)
                xbc_a = jnp.concatenate([xa_ref[j] for j in range(N_SLABS)], axis=1)
            else:
                big = jnp.concatenate([xbc_ref[j, b, r0:r0 + ch + CONV_HDR, :] for j in range(N_SLABS)], axis=1)
                acc = cb_ref[...] + cw_ref[CONV_W - 1:CONV_W, :] * big[CONV_HDR:, :]
                for k in range(CONV_W - 1):
                    acc = acc + cw_ref[k:k + 1, :] * pltpu.roll(big, CONV_W - 1 - k, 0)[CONV_HDR:, :]
                xbc_a = _silu(acc)
            xs = xbc_a[:, :SSD_WIDTH]
            xsb = xs.astype(BF16)
            bmb = xbc_a[:, SSD_WIDTH:SSD_WIDTH + SSD_GROUPS * D_STATE].astype(BF16)
            cmb = xbc_a[:, SSD_WIDTH + SSD_GROUPS * D_STATE:].astype(BF16)

            dtr = dt_ref[rs, :] + dtb_ref[...]
            dt = jnp.maximum(dtr, 0.0) + jnp.log1p(jnp.exp(-jnp.abs(dtr)))
            acum = cumsum_rows(dt * a_row)
            acum_e = expand_heads(acum)
            ea_e = jnp.exp(acum_e)
            xw = (xs * (jnp.exp(acum_e[ch - 1:ch, :] - acum_e) * expand_heads(dt))).astype(BF16)
            acum_t = transpose_rows(acum)
            dt_t = transpose_rows(dt)

            if nbt > 1:
                for g in range(SSD_GROUPS):
                    s_ref[g] = sinit_ref[b, g * GROUP_W:(g + 1) * GROUP_W, :].T

            ys = []
            for g in range(SSD_GROUPS):
                gs = slice(g * GROUP_W, (g + 1) * GROUP_W)
                cg = cmb[:, g * D_STATE:(g + 1) * D_STATE]
                bg = bmb[:, g * D_STATE:(g + 1) * D_STATE]
                cbm = lax.dot_general(cg, bg, (((1,), (1,)), ((), ())), preferred_element_type=F32)
                mms, rhss = [], []
                for r in range(SSD_HPG):
                    hh = g * SSD_HPG + r
                    dec = jnp.where(causal, jnp.exp(acum[:, hh:hh + 1] - acum_t[hh:hh + 1, :]), 0.0)
                    mms.append((cbm * (dec * dt_t[hh:hh + 1, :])).astype(BF16))
                    rhss.append(jnp.where(head_of_lane == r, xsb[:, gs], jnp.zeros((ch, GROUP_W), BF16)))
                if stack_k:
                    mms = [jnp.concatenate(mms[2 * q:2 * q + 2], axis=1) for q in range(SSD_HPG // 2)]
                    rhss = [jnp.concatenate(rhss[2 * q:2 * q + 2], axis=0) for q in range(SSD_HPG // 2)]
                yg = None
                for mm, rhs in zip(mms, rhss):
                    part = _dot(mm, rhs)
                    yg = part if yg is None else yg + part
                sg = s_ref[g]
                yg = yg + _dot(cg, sg.astype(BF16)) * ea_e[:, gs]
                ds = lax.dot_general(bg, xw[:, gs], (((0,), (0,)), ((), ())), preferred_element_type=F32)
                s_ref[g] = ea_e[ch - 1:ch, gs] * sg + ds
                ys.append(yg)
            y = jnp.concatenate(ys, axis=1) + dskip_ref[...] * xs
            y = _rms(y * z_ref[rs, :]) * sng_ref[...]
            y_ref[rs, 0:SSD_WIDTH] = y.astype(BF16)

            if nbt > 1:
                for g in range(SSD_GROUPS):
                    sout_ref[b, g * GROUP_W:(g + 1) * GROUP_W, :] = s_ref[g].T

            u = u_ref[rs, :]
            v = v_ref[rs, :]
            vc = v - jnp.mean(v, axis=-1, keepdims=True)
            vn = vc * lax.rsqrt(jnp.mean(vc * vc, axis=-1, keepdims=True) + LN_EPS) * vg_ref[...] + vb_ref[...]
            if emit_v:
                vout_ref[b, r0:r0 + ch, :] = vn
            vnb = vn.astype(BF16)
            gw = 2 * CMLP_GDIM if stack_k else CMLP_GDIM
            for q, wq in enumerate(w_mix):
                sl = slice(q * gw, (q + 1) * gw)
                if stack_k:
                    va, vb_ = vnb[:, q * gw:q * gw + CMLP_GDIM], vnb[:, q * gw + CMLP_GDIM:(q + 1) * gw]
                    rhs = jnp.concatenate([jnp.concatenate([va, zeros_gd], axis=1),
                                           jnp.concatenate([zeros_gd, vb_], axis=1)], axis=0)
                else:
                    rhs = vnb[:, sl]
                mixed = _dot(wq, rhs) + bs_ref[:, sl]
                y_ref[rs, SSD_WIDTH + q * gw:SSD_WIDTH + (q + 1) * gw] = (u[:, sl] * mixed).astype(BF16)
            yield

        for b in range(nbt):
            for j in range(N_SLABS):
                next_slot[0][j, b, 0:CONV_HDR, :] = xbc_ref[j, b, lt:lt + CONV_HDR, :]

    def emit(slot):
        y_ref = slot[5]
        x = xc_ref[...].reshape(tm, D_MODEL)
        g1 = modc_ref[...][:, 2, :]
        m = _dot(y_ref[...], wout_ref[...])
        yield
        x1 = x + rows(g1 * ng_ref[1:2, :]) * _rms(m)
        x1_ref[...] = x1.reshape(nbt, lt, D_MODEL)

    def run(gens, order):
        for k in order:
            next(gens[k], None)
        for g in gens:
            for _ in g:
                pass

    if not pipelined:
        assert nt == 1
        init_tile(slots[0])
        run([project(slots[0]), mix(slots[0], slots[0]), emit(slots[0])], ())
        finish_tile(slots[0])
        return

    assert nbt == 1
    tb = jnp.clip(s_id - 1, 0, n_tiles - 1) % nt
    mix_is_real = jnp.logical_and(s_id >= 1, s_id <= n_tiles)

    @pl.when(s_id == 0)
    def _():
        for ref in slots[1]:
            ref[...] = jnp.zeros(ref.shape, ref.dtype)
        slots[0][5][...] = jnp.zeros(slots[0][5].shape, BF16)

    for par in range(2):
        @pl.when(s_id % 2 == par)
        def _(par=par):
            mine, other = slots[1 - par], slots[par]
            pl.when(tb == 0)(lambda: init_tile(mine))
            run([project(other), mix(mine, other), emit(other)], (2, 0, 1, 0, 0, 2, 1, 0, 1, 0, 1, 0))
            pl.when(jnp.logical_and(tb == nt - 1, mix_is_real))(lambda: finish_tile(mine))


def _mixer_call(x, mod, cinit, sinit, ng, win, cw, cb, dtb, alog, dskip, sng, vg, vb, ws, bs,
                wout, *, nbt, lt, ch, emit_v):
    nb, seq, _ = x.shape
    tm = nbt * lt
    nt = seq // lt
    n_tiles = (nb // nbt) * nt
    pipelined = n_tiles > 1
    lag = 1 if pipelined else 0
    n_steps = n_tiles + 2 * lag

    def tile_a(s):
        return jnp.clip(s, 0, n_tiles - 1)

    def tile_b(s):
        return jnp.clip(s - lag, 0, n_tiles - 1)

    def tile_c(s):
        return jnp.clip(s - 2 * lag, 0, n_tiles - 1)

    def const(shape):
        return pl.BlockSpec(shape, lambda s: (0,) * len(shape))

    def per_seq(shape, tile):
        return pl.BlockSpec(shape, lambda s: (tile(s) // nt, 0, 0))

    def per_tile(shape, tile):
        return pl.BlockSpec(shape, lambda s: (tile(s) // nt, tile(s) % nt, 0))

    in_specs = [
        per_tile((nbt, lt, D_MODEL), tile_a),
        per_tile((nbt, lt, D_MODEL), tile_c),
        per_seq((nbt, N_MOD, D_MODEL), tile_a),
        per_seq((nbt, N_MOD, D_MODEL), tile_c),
        per_seq((nbt, CONV_HDR, CONV_DIM), tile_b),
        per_seq((nbt, SSD_WIDTH, D_STATE), tile_b),
        const((4, D_MODEL)),
        const((D_MODEL, PROJ_W)),
        const((CONV_W, CONV_DIM)),
        const((1, CONV_DIM)),
        const((1, LANES)),
        const((1, LANES)),
        const((1, SSD_WIDTH)),
        const((1, SSD_WIDTH)),
        const((1, CMLP_WIDTH)),
        const((1, CMLP_WIDTH)),
        const((CMLP_GROUPS, ch, ch)),
        const((ch, CMLP_WIDTH)),
        const((D_MODEL, D_MODEL)),
    ]
    out_specs = [
        per_tile((nbt, lt, D_MODEL), tile_c),
        per_seq((nbt, CONV_HDR, CONV_DIM), tile_b),
        per_seq((nbt, SSD_WIDTH, D_STATE), tile_b),
    ]
    out_shape = [
        jax.ShapeDtypeStruct((nb, seq, D_MODEL), F32),
        jax.ShapeDtypeStruct((nb, CONV_HDR, CONV_DIM), F32),
        jax.ShapeDtypeStruct((nb, SSD_WIDTH, D_STATE), F32),
    ]
    if emit_v:
        out_specs.append(per_tile((nbt, lt, CMLP_WIDTH), tile_b))
        out_shape.append(jax.ShapeDtypeStruct((nb, seq, CMLP_WIDTH), F32))
    slot = [
        pltpu.VMEM((N_SLABS, nbt, lt + CONV_HDR, LANES), F32),
        pltpu.VMEM((tm, SSD_WIDTH), F32),
        pltpu.VMEM((tm, LANES), F32),
        pltpu.VMEM((tm, CMLP_WIDTH), F32),
        pltpu.VMEM((tm, CMLP_WIDTH), F32),
        pltpu.VMEM((tm, D_MODEL), BF16),
    ]
    assert len(slot) == N_SLOT_BUFS
    scratch = slot * (2 if pipelined else 1) + [
        pltpu.VMEM((SSD_GROUPS, D_STATE, GROUP_W), F32),
    ]
    if ch == LANES:
        scratch += [pltpu.VMEM((N_SLABS, ch, LANES), F32)] * (tm // ch)
    return pl.pallas_call(
        functools.partial(_mixer_kernel, nbt=nbt, lt=lt, ch=ch, nt=nt, n_tiles=n_tiles, pipelined=pipelined,
                          emit_v=emit_v),
        grid=(n_steps,),
        in_specs=in_specs,
        out_specs=out_specs,
        out_shape=out_shape,
        scratch_shapes=scratch,
        compiler_params=pltpu.CompilerParams(
            dimension_semantics=("arbitrary",), vmem_limit_bytes=VMEM_LIMIT),
        name="mixer",
    )(x, x, mod, mod, cinit, sinit, ng, win, cw, cb, dtb, alog, dskip, sng, vg, vb, ws, bs, wout)


FF_CHUNK = 1024


def _ffn_kernel(x_ref, mod_ref, ng_ref, w1_ref, w2_ref, o_ref, *, nbt, lt):
    tm = nbt * lt

    def rows(m):
        if nbt == 1:
            return m
        return jnp.broadcast_to(m[:, None, :], (nbt, lt, m.shape[-1])).reshape(tm, m.shape[-1])

    x = x_ref[...].reshape(tm, D_MODEL)
    mod = mod_ref[...]
    sh2, sc2, g2 = mod[:, 3, :], mod[:, 4, :], mod[:, 5, :]
    h = _rms(x) * rows(ng_ref[2:3, :] * (1.0 + sc2)) + rows(sh2)
    hb = h.astype(BF16)
    f = jnp.zeros((tm, D_MODEL), F32)
    for k in range(D_FF // FF_CHUNK):
        a = jnp.maximum(_dot(hb, w1_ref[:, k * FF_CHUNK:(k + 1) * FF_CHUNK]), 0.0)
        f = f + _dot((a * a).astype(BF16), w2_ref[k * FF_CHUNK:(k + 1) * FF_CHUNK, :])
    o = x + rows(g2) * (_rms(f) * ng_ref[3:4, :])
    o_ref[...] = o.reshape(nbt, lt, D_MODEL)


def _ffn_call(x, mod, ng, w1, w2, *, nbt, lt):
    nb, seq, _ = x.shape
    return pl.pallas_call(
        functools.partial(_ffn_kernel, nbt=nbt, lt=lt),
        grid=(nb // nbt, seq // lt),
        in_specs=[
            pl.BlockSpec((nbt, lt, D_MODEL), lambda bi, ti: (bi, ti, 0)),
            pl.BlockSpec((nbt, N_MOD, D_MODEL), lambda bi, ti: (bi, 0, 0)),
            pl.BlockSpec((4, D_MODEL), lambda bi, ti: (0, 0)),
            pl.BlockSpec((D_MODEL, D_FF), lambda bi, ti: (0, 0)),
            pl.BlockSpec((D_FF, D_MODEL), lambda bi, ti: (0, 0)),
        ],
        out_specs=pl.BlockSpec((nbt, lt, D_MODEL), lambda bi, ti: (bi, ti, 0)),
        out_shape=jax.ShapeDtypeStruct((nb, seq, D_MODEL), F32),
        compiler_params=pltpu.CompilerParams(
            dimension_semantics=("arbitrary", "arbitrary"), vmem_limit_bytes=VMEM_LIMIT),
        name="ffn",
    )(x, mod, ng, w1, w2)


def _tile_plan(nb, seq):
    if seq >= 512:
        return 1, 512, CMLP_CHUNK
    assert seq <= CMLP_CHUNK and seq % 16 == 0 and (nb * seq) % 8 == 0
    return nb, seq, seq


def _layer_params(l, w_in, w_out, w_ff1, w_ff2, conv_w, conv_b, dt_bias, a_log, d_skip,
                  ssd_norm_g, v_ln_g, v_ln_b):
    wi = w_in[l]
    z0, x0, d0, u0 = 0, SSD_WIDTH, SSD_WIDTH + CONV_DIM, SSD_WIDTH + CONV_DIM + SSD_HEADS
    reps = LANES // SSD_HEADS
    win = jnp.concatenate([
        wi[:, z0:x0], wi[:, x0:d0], wi[:, u0:u0 + CMLP_WIDTH], wi[:, u0 + CMLP_WIDTH:],
        jnp.tile(wi[:, d0:u0], (1, reps))], axis=1).astype(BF16)
    return dict(
        win=win, wout=w_out[l].astype(BF16), w1=w_ff1[l].astype(BF16), w2=w_ff2[l].astype(BF16),
        cw=conv_w[l], cb=conv_b[l].reshape(1, CONV_DIM),
        dtb=jnp.tile(dt_bias[l], reps).reshape(1, LANES),
        alog=jnp.tile(a_log[l], reps).reshape(1, LANES),
        dskip=jnp.repeat(d_skip[l], SSD_HEAD_DIM).reshape(1, SSD_WIDTH),
        sng=ssd_norm_g[l].reshape(1, SSD_WIDTH),
        vg=v_ln_g[l].reshape(1, CMLP_WIDTH), vb=v_ln_b[l].reshape(1, CMLP_WIDTH))


def _block(x, mod, cinit, sinit, p, ng, ws_l, bs_l, emit_v):
    nb, seq, _ = x.shape
    nbt, lt, ch = _tile_plan(nb, seq)
    ws = ws_l[:, :ch, :ch]
    bs = jnp.repeat(bs_l[:, :ch].T, CMLP_GDIM, axis=1)
    cin8 = jnp.pad(cinit, ((0, 0), (CONV_HDR - (CONV_W - 1), 0), (0, 0)))
    outs = _mixer_call(x, mod, cin8, sinit.reshape(nb, SSD_WIDTH, D_STATE), ng, p["win"], p["cw"], p["cb"],
                       p["dtb"], p["alog"], p["dskip"], p["sng"], p["vg"], p["vb"], ws, bs, p["wout"],
                       nbt=nbt, lt=lt, ch=ch, emit_v=emit_v)
    x1, cout, sout = outs[:3]
    x2 = _ffn_call(x1, mod, ng, p["w1"], p["w2"], nbt=nbt, lt=lt)
    conv_state = cout[:, CONV_HDR - (CONV_W - 1):, :]
    ssd_state = sout.reshape(nb, SSD_HEADS, SSD_HEAD_DIM, D_STATE)
    return x2, conv_state, ssd_state, (outs[3] if emit_v else None)


def kernel(x_prompt, x_sample, state_conv, state_ssd, c_prompt, c_sample, w_mod, b_mod, norm_g, w_in, conv_w,
           conv_b, dt_bias, a_log, d_skip, ssd_norm_g, v_ln_g, v_ln_b, w_s, b_s, w_out, w_ff1, w_ff2):
    bp = x_prompt.shape[0]
    bs_ = x_sample.shape[0]
    mod = _mod_call(jnp.concatenate([c_prompt, c_sample], axis=0), w_mod, b_mod)
    mod = mod.reshape(DEPTH, bp + bs_, N_MOD, D_MODEL)
    conv_p0 = jnp.zeros((bp, CONV_W - 1, CONV_DIM), F32)
    ssd_p0 = jnp.zeros((bp, SSD_HEADS, SSD_HEAD_DIM, D_STATE), F32)
    xp, xs = x_prompt, x_sample
    conv_p, ssd_p, conv_s, ssd_s, v_s = [], [], [], [], []
    for l in range(DEPTH):
        p = _layer_params(l, w_in, w_out, w_ff1, w_ff2, conv_w, conv_b, dt_bias, a_log, d_skip,
                          ssd_norm_g, v_ln_g, v_ln_b)
        xp, cp, hp, _ = _block(xp, mod[l, :bp], conv_p0, ssd_p0, p, norm_g[l], w_s[l], b_s[l], False)
        xs, cs, hs, vs = _block(xs, mod[l, bp:], state_conv[l], state_ssd[l], p, norm_g[l], w_s[l], b_s[l], True)
        conv_p.append(cp)
        ssd_p.append(hp)
        conv_s.append(cs)
        ssd_s.append(hs)
        v_s.append(vs)
    return (xp, xs, jnp.stack(conv_p), jnp.stack(ssd_p), jnp.stack(conv_s), jnp.stack(ssd_s), jnp.stack(v_s))
```

```python
import functools

import numpy as np
import jax
import jax.numpy as jnp
from jax import lax
from jax.experimental import pallas as pl
from jax.experimental.pallas import tpu as pltpu

F32 = jnp.float32
BF16 = jnp.bfloat16

D_MODEL = 1024
DEPTH = 4
SSD_WIDTH = 512
SSD_HEAD_DIM = 64
SSD_HEADS = 8
SSD_GROUPS = 2
SSD_HPG = 4
D_STATE = 128
CONV_W = 4
CONV_DIM = 1024
CMLP_WIDTH = 512
CMLP_GROUPS = 4
CMLP_GDIM = 128
CMLP_CHUNK = 128
D_FF = 4096
N_MOD = 6
RMS_EPS = 1e-6
LN_EPS = 1e-5

LANES = 128
CONV_HDR = 8
N_SLABS = CONV_DIM // LANES
GROUP_W = SSD_HPG * SSD_HEAD_DIM
OFF_Z, OFF_XBC, OFF_U, OFF_V, OFF_DT = 0, 512, 1536, 2048, 2560
PROJ_W = OFF_DT + LANES
VMEM_LIMIT = 56 * 1024 * 1024


def _silu(x):
    h = 0.5 * x
    return h + h * jnp.tanh(h)


GELU_C0 = float(np.sqrt(2.0 / np.pi))
GELU_C1 = GELU_C0 * 0.044715


def _gelu_tanh(x):
    h = 0.5 * x
    return h + h * jnp.tanh(x * (GELU_C0 + GELU_C1 * (x * x)))


def _rms(x, eps=RMS_EPS):
    return x * lax.rsqrt(jnp.mean(x * x, axis=-1, keepdims=True) + eps)


def _dot(a, b):
    return jnp.dot(a, b, preferred_element_type=F32)


def _mod_kernel(c_ref, w_ref, b_ref, o_ref):
    s = _silu(c_ref[...]).astype(BF16)
    o_ref[0] = _dot(s, w_ref[0].astype(BF16)) + b_ref[0]


def _mod_call(c_all, w_mod, b_mod):
    nb = c_all.shape[0]
    tn = 1536
    n = N_MOD * D_MODEL
    return pl.pallas_call(
        _mod_kernel,
        grid=(DEPTH, n // tn),
        in_specs=[
            pl.BlockSpec((nb, D_MODEL), lambda l, j: (0, 0)),
            pl.BlockSpec((1, D_MODEL, tn), lambda l, j: (l, 0, j)),
            pl.BlockSpec((1, 1, tn), lambda l, j: (l, 0, j)),
        ],
        out_specs=pl.BlockSpec((1, nb, tn), lambda l, j: (l, 0, j)),
        out_shape=jax.ShapeDtypeStruct((DEPTH, nb, n), F32),
        compiler_params=pltpu.CompilerParams(
            dimension_semantics=("arbitrary", "arbitrary"), vmem_limit_bytes=VMEM_LIMIT),
        name="mod",
    )(c_all, w_mod, b_mod.reshape(DEPTH, 1, n))


N_SLOT_BUFS = 6


def _mixer_kernel(xa_ref, xc_ref, moda_ref, modc_ref, cinit_ref, sinit_ref, ng_ref, win_ref, cw_ref, cb_ref,
                  dtb_ref, alog_ref, dskip_ref, sng_ref, vg_ref, vb_ref, ws_ref, bs_ref,
                  wout_ref,
                  *refs, nbt, lt, ch, nt, n_tiles, pipelined, emit_v):
    n_out = 4 if emit_v else 3
    x1_ref, cout_ref, sout_ref = refs[:3]
    vout_ref = refs[3] if emit_v else None
    scratch = refs[n_out:]
    slots = [scratch[k * N_SLOT_BUFS:(k + 1) * N_SLOT_BUFS] for k in range(2 if pipelined else 1)]
    s_ref, *xa_refs = scratch[len(slots) * N_SLOT_BUFS:]
    stack_k = ch == LANES
    strided_conv = len(xa_refs) > 0

    s_id = pl.program_id(0)
    tm = nbt * lt
    nc = lt // ch

    def rows(m):
        if nbt == 1:
            return m
        return jnp.broadcast_to(m[:, None, :], (nbt, lt, m.shape[-1])).reshape(tm, m.shape[-1])

    n_sub = ch // 8
    sub_i = lax.broadcasted_iota(jnp.int32, (n_sub, 8, LANES), 1)
    lane_i = lax.broadcasted_iota(jnp.int32, (ch, LANES), 1)
    ii = lax.broadcasted_iota(jnp.int32, (ch, ch), 0)
    jj = lax.broadcasted_iota(jnp.int32, (ch, ch), 1)
    causal = ii >= jj
    head_of_lane = lax.broadcasted_iota(jnp.int32, (ch, GROUP_W), 1) // SSD_HEAD_DIM

    def expand_heads(v):
        cols = [jnp.broadcast_to(v[:, h:h + 1], (ch, LANES)) for h in range(SSD_HEADS)]
        return jnp.concatenate([jnp.where(lane_i < SSD_HEAD_DIM, cols[2 * q], cols[2 * q + 1])
                                for q in range(SSD_HEADS // 2)], axis=1)

    def transpose_rows(v):
        if ch == LANES:
            return v.T
        return jnp.concatenate([v, jnp.zeros((LANES - ch, LANES), F32)], axis=0).T[:, :ch]

    def cumsum_rows(a):
        a3 = a.reshape(n_sub, 8, LANES)
        for sft in (1, 2, 4):
            a3 = a3 + jnp.where(sub_i >= sft, pltpu.roll(a3, sft, 1), 0.0)
        carry = jnp.zeros((1, LANES), F32)
        out = []
        for k in range(n_sub):
            out.append(a3[k] + carry)
            carry = carry + a3[k, 7:8, :]
        return jnp.concatenate(out, axis=0)

    def project(slot):
        xbc_ref, z_ref, dt_ref, u_ref, v_ref, _ = slot
        x = xa_ref[...].reshape(tm, D_MODEL)
        mod = moda_ref[...]
        sh1, sc1 = mod[:, 0, :], mod[:, 1, :]
        h = _rms(x) * rows(ng_ref[0:1, :] * (1.0 + sc1)) + rows(sh1)
        hb = h.astype(BF16)
        yield
        z_ref[...] = _silu(_dot(hb, win_ref[:, OFF_Z:OFF_XBC]))
        yield
        xbc = _dot(hb, win_ref[:, OFF_XBC:OFF_U])
        for b in range(nbt):
            for j in range(N_SLABS):
                xbc_ref[j, b, CONV_HDR:CONV_HDR + lt, :] = xbc[b * lt:(b + 1) * lt, j * LANES:(j + 1) * LANES]
        yield
        u_ref[...] = _gelu_tanh(_dot(hb, win_ref[:, OFF_U:OFF_V]))
        yield
        vd = _dot(hb, win_ref[:, OFF_V:PROJ_W])
        v_ref[...] = _gelu_tanh(vd[:, :CMLP_WIDTH])
        dt_ref[...] = vd[:, CMLP_WIDTH:]
        yield

    def init_tile(slot):
        xbc_ref = slot[0]
        for b in range(nbt):
            for j in range(N_SLABS):
                xbc_ref[j, b, 0:CONV_HDR, :] = cinit_ref[b, :, j * LANES:(j + 1) * LANES]
        if nbt == 1:
            for g in range(SSD_GROUPS):
                s_ref[g] = sinit_ref[0, g * GROUP_W:(g + 1) * GROUP_W, :].T

    def finish_tile(slot):
        xbc_ref = slot[0]
        for b in range(nbt):
            for j in range(N_SLABS):
                cout_ref[b, :, j * LANES:(j + 1) * LANES] = xbc_ref[j, b, lt:lt + CONV_HDR, :]
        if nbt == 1:
            for g in range(SSD_GROUPS):
                sout_ref[0, g * GROUP_W:(g + 1) * GROUP_W, :] = s_ref[g].T

    def mix(slot, next_slot):
        xbc_ref, z_ref, dt_ref, u_ref, v_ref, y_ref = slot
        a_row = -jnp.exp(alog_ref[...])
        w_mix = [jnp.where(causal, ws_ref[g], 0.0).astype(BF16) for g in range(CMLP_GROUPS)]
        if stack_k:
            w_mix = [jnp.concatenate(w_mix[2 * q:2 * q + 2], axis=1) for q in range(CMLP_GROUPS // 2)]
        zeros_gd = jnp.zeros((ch, CMLP_GDIM), BF16)
        for i in range(nbt * nc):
            b, c = divmod(i, nc)
            r0 = c * ch
            rs = slice(i * ch, (i + 1) * ch)

            if strided_conv:
                xa_ref = xa_refs[i]
                first = CONV_HDR - (CONV_W - 1)
                for j in range(N_SLABS):
                    col = slice(j * LANES, (j + 1) * LANES)
                    phase = {m: xbc_ref[j, b, pl.ds(r0 + m, n_sub, stride=8), :]
                             for m in range(first, first + 8 + CONV_W - 1)}
                    for p in range(8):
                        acc = cb_ref[:, col]
                        for k in range(CONV_W):
                            acc = acc + cw_ref[k:k + 1, col] * phase[first + p + k]
                        xa_ref[j, pl.ds(p, n_sub, stride=8), :] = _silu(acc)
                xbc_a = jnp.concatenate([xa_ref[j] for j in range(N_SLABS)], axis=1)
            else:
                big = jnp.concatenate([xbc_ref[j, b, r0:r0 + ch + CONV_HDR, :] for j in range(N_SLABS)], axis=1)
                acc = cb_ref[...] + cw_ref[CONV_W - 1:CONV_W, :] * big[CONV_HDR:, :]
                for k in range(CONV_W - 1):
                    acc = acc + cw_ref[k:k + 1, :] * pltpu.roll(big, CONV_W - 1 - k, 0)[CONV_HDR:, :]
                xbc_a = _silu(acc)
            xs = xbc_a[:, :SSD_WIDTH]
            xsb = xs.astype(BF16)
            bmb = xbc_a[:, SSD_WIDTH:SSD_WIDTH + SSD_GROUPS * D_STATE].astype(BF16)
            cmb = xbc_a[:, SSD_WIDTH + SSD_GROUPS * D_STATE:].astype(BF16)

            dtr = dt_ref[rs, :] + dtb_ref[...]
            dt = jnp.maximum(dtr, 0.0) + jnp.log1p(jnp.exp(-jnp.abs(dtr)))
            acum = cumsum_rows(dt * a_row)
            acum_e = expand_heads(acum)
            ea_e = jnp.exp(acum_e)
            xw = (xs * (jnp.exp(acum_e[ch - 1:ch, :] - acum_e) * expand_heads(dt))).astype(BF16)
            acum_t = transpose_rows(acum)
            dt_t = transpose_rows(dt)

            if nbt > 1:
                for g in range(SSD_GROUPS):
                    s_ref[g] = sinit_ref[b, g * GROUP_W:(g + 1) * GROUP_W, :].T

            ys = []
            for g in range(SSD_GROUPS):
                gs = slice(g * GROUP_W, (g + 1) * GROUP_W)
                cg = cmb[:, g * D_STATE:(g + 1) * D_STATE]
                bg = bmb[:, g * D_STATE:(g + 1) * D_STATE]
                cbm = lax.dot_general(cg, bg, (((1,), (1,)), ((), ())), preferred_element_type=F32)
                mms, rhss = [], []
                for r in range(SSD_HPG):
                    hh = g * SSD_HPG + r
                    dec = jnp.where(causal, jnp.exp(acum[:, hh:hh + 1] - acum_t[hh:hh + 1, :]), 0.0)
                    mms.append((cbm * (dec * dt_t[hh:hh + 1, :])).astype(BF16))
                    rhss.append(jnp.where(head_of_lane == r, xsb[:, gs], jnp.zeros((ch, GROUP_W), BF16)))
                if stack_k:
                    mms = [jnp.concatenate(mms[2 * q:2 * q + 2], axis=1) for q in range(SSD_HPG // 2)]
                    rhss = [jnp.concatenate(rhss[2 * q:2 * q + 2], axis=0) for q in range(SSD_HPG // 2)]
                yg = None
                for mm, rhs in zip(mms, rhss):
                    part = _dot(mm, rhs)
                    yg = part if yg is None else yg + part
                sg = s_ref[g]
                yg = yg + _dot(cg, sg.astype(BF16)) * ea_e[:, gs]
                ds = lax.dot_general(bg, xw[:, gs], (((0,), (0,)), ((), ())), preferred_element_type=F32)
                s_ref[g] = ea_e[ch - 1:ch, gs] * sg + ds
                ys.append(yg)
            y = jnp.concatenate(ys, axis=1) + dskip_ref[...] * xs
            y = _rms(y * z_ref[rs, :]) * sng_ref[...]
            y_ref[rs, 0:SSD_WIDTH] = y.astype(BF16)

            if nbt > 1:
                for g in range(SSD_GROUPS):
                    sout_ref[b, g * GROUP_W:(g + 1) * GROUP_W, :] = s_ref[g].T

            u = u_ref[rs, :]
            v = v_ref[rs, :]
            vc = v - jnp.mean(v, axis=-1, keepdims=True)
            vn = vc * lax.rsqrt(jnp.mean(vc * vc, axis=-1, keepdims=True) + LN_EPS) * vg_ref[...] + vb_ref[...]
            if emit_v:
                vout_ref[b, r0:r0 + ch, :] = vn
            vnb = vn.astype(BF16)
            gw = 2 * CMLP_GDIM if stack_k else CMLP_GDIM
            for q, wq in enumerate(w_mix):
                sl = slice(q * gw, (q + 1) * gw)
                if stack_k:
                    va, vb_ = vnb[:, q * gw:q * gw + CMLP_GDIM], vnb[:, q * gw + CMLP_GDIM:(q + 1) * gw]
                    rhs = jnp.concatenate([jnp.concatenate([va, zeros_gd], axis=1),
                                           jnp.concatenate([zeros_gd, vb_], axis=1)], axis=0)
                else:
                    rhs = vnb[:, sl]
                mixed = _dot(wq, rhs) + bs_ref[:, sl]
                y_ref[rs, SSD_WIDTH + q * gw:SSD_WIDTH + (q + 1) * gw] = (u[:, sl] * mixed).astype(BF16)
            yield

        for b in range(nbt):
            for j in range(N_SLABS):
                next_slot[0][j, b, 0:CONV_HDR, :] = xbc_ref[j, b, lt:lt + CONV_HDR, :]

    def emit(slot):
        y_ref = slot[5]
        x = xc_ref[...].reshape(tm, D_MODEL)
        g1 = modc_ref[...][:, 2, :]
        m = _dot(y_ref[...], wout_ref[...])
        yield
        x1 = x + rows(g1 * ng_ref[1:2, :]) * _rms(m)
        x1_ref[...] = x1.reshape(nbt, lt, D_MODEL)

    def run(gens, order):
        for k in order:
            next(gens[k], None)
        for g in gens:
            for _ in g:
                pass

    if not pipelined:
        assert nt == 1
        init_tile(slots[0])
        run([project(slots[0]), mix(slots[0], slots[0]), emit(slots[0])], ())
        finish_tile(slots[0])
        return

    assert nbt == 1
    tb = jnp.clip(s_id - 1, 0, n_tiles - 1) % nt
    mix_is_real = jnp.logical_and(s_id >= 1, s_id <= n_tiles)

    @pl.when(s_id == 0)
    def _():
        for ref in slots[1]:
            ref[...] = jnp.zeros(ref.shape, ref.dtype)
        slots[0][5][...] = jnp.zeros(slots[0][5].shape, BF16)

    for par in range(2):
        @pl.when(s_id % 2 == par)
        def _(par=par):
            mine, other = slots[1 - par], slots[par]
            pl.when(tb == 0)(lambda: init_tile(mine))
            run([project(other), mix(mine, other), emit(other)], (2, 0, 1, 0, 0, 2, 1, 0, 1, 0, 1, 0))
            pl.when(jnp.logical_and(tb == nt - 1, mix_is_real))(lambda: finish_tile(mine))


def _mixer_call(x, mod, cinit, sinit, ng, win, cw, cb, dtb, alog, dskip, sng, vg, vb, ws, bs,
                wout, *, nbt, lt, ch, emit_v):
    nb, seq, _ = x.shape
    tm = nbt * lt
    nt = seq // lt
    n_tiles = (nb // nbt) * nt
    pipelined = n_tiles > 1
    lag = 1 if pipelined else 0
    n_steps = n_tiles + 2 * lag

    def tile_a(s):
        return jnp.clip(s, 0, n_tiles - 1)

    def tile_b(s):
        return jnp.clip(s - lag, 0, n_tiles - 1)

    def tile_c(s):
        return jnp.clip(s - 2 * lag, 0, n_tiles - 1)

    def const(shape):
        return pl.BlockSpec(shape, lambda s: (0,) * len(shape))

    def per_seq(shape, tile):
        return pl.BlockSpec(shape, lambda s: (tile(s) // nt, 0, 0))

    def per_tile(shape, tile):
        return pl.BlockSpec(shape, lambda s: (tile(s) // nt, tile(s) % nt, 0))

    in_specs = [
        per_tile((nbt, lt, D_MODEL), tile_a),
        per_tile((nbt, lt, D_MODEL), tile_c),
        per_seq((nbt, N_MOD, D_MODEL), tile_a),
        per_seq((nbt, N_MOD, D_MODEL), tile_c),
        per_seq((nbt, CONV_HDR, CONV_DIM), tile_b),
        per_seq((nbt, SSD_WIDTH, D_STATE), tile_b),
        const((4, D_MODEL)),
        const((D_MODEL, PROJ_W)),
        const((CONV_W, CONV_DIM)),
        const((1, CONV_DIM)),
        const((1, LANES)),
        const((1, LANES)),
        const((1, SSD_WIDTH)),
        const((1, SSD_WIDTH)),
        const((1, CMLP_WIDTH)),
        const((1, CMLP_WIDTH)),
        const((CMLP_GROUPS, ch, ch)),
        const((ch, CMLP_WIDTH)),
        const((D_MODEL, D_MODEL)),
    ]
    out_specs = [
        per_tile((nbt, lt, D_MODEL), tile_c),
        per_seq((nbt, CONV_HDR, CONV_DIM), tile_b),
        per_seq((nbt, SSD_WIDTH, D_STATE), tile_b),
    ]
    out_shape = [
        jax.ShapeDtypeStruct((nb, seq, D_MODEL), F32),
        jax.ShapeDtypeStruct((nb, CONV_HDR, CONV_DIM), F32),
        jax.ShapeDtypeStruct((nb, SSD_WIDTH, D_STATE), F32),
    ]
    if emit_v:
        out_specs.append(per_tile((nbt, lt, CMLP_WIDTH), tile_b))
        out_shape.append(jax.ShapeDtypeStruct((nb, seq, CMLP_WIDTH), F32))
    slot = [
        pltpu.VMEM((N_SLABS, nbt, lt + CONV_HDR, LANES), F32),
        pltpu.VMEM((tm, SSD_WIDTH), F32),
        pltpu.VMEM((tm, LANES), F32),
        pltpu.VMEM((tm, CMLP_WIDTH), F32),
        pltpu.VMEM((tm, CMLP_WIDTH), F32),
        pltpu.VMEM((tm, D_MODEL), BF16),
    ]
    assert len(slot) == N_SLOT_BUFS
    scratch = slot * (2 if pipelined else 1) + [
        pltpu.VMEM((SSD_GROUPS, D_STATE, GROUP_W), F32),
    ]
    if ch == LANES:
        scratch += [pltpu.VMEM((N_SLABS, ch, LANES), F32)] * (tm // ch)
    return pl.pallas_call(
        functools.partial(_mixer_kernel, nbt=nbt, lt=lt, ch=ch, nt=nt, n_tiles=n_tiles, pipelined=pipelined,
                          emit_v=emit_v),
        grid=(n_steps,),
        in_specs=in_specs,
        out_specs=out_specs,
        out_shape=out_shape,
        scratch_shapes=scratch,
        compiler_params=pltpu.CompilerParams(
            dimension_semantics=("arbitrary",), vmem_limit_bytes=VMEM_LIMIT),
        name="mixer",
    )(x, x, mod, mod, cinit, sinit, ng, win, cw, cb, dtb, alog, dskip, sng, vg, vb, ws, bs, wout)


FF_CHUNK = 1024


FFN_ROWS = 256
FFN_TILE = 1024


def _ffn_kernel(x_ref, mod_ref, ng_ref, w1_ref, w2_ref, o_ref, *, nbt, lt):
    tm = nbt * lt

    def rows(m):
        if nbt == 1:
            return m
        return jnp.broadcast_to(m[:, None, :], (nbt, lt, m.shape[-1])).reshape(tm, m.shape[-1])

    mod = mod_ref[...]
    sh2, sc2, g2 = mod[:, 3, :], mod[:, 4, :], mod[:, 5, :]
    n_blk = lt // FFN_ROWS if (nbt == 1 and lt % FFN_ROWS == 0) else 1
    for k_blk in range(n_blk):
        if n_blk == 1:
            x = x_ref[...].reshape(tm, D_MODEL)
        else:
            r = slice(k_blk * FFN_ROWS, (k_blk + 1) * FFN_ROWS)
            x = x_ref[0, r, :]
        h = _rms(x) * rows(ng_ref[2:3, :] * (1.0 + sc2)) + rows(sh2)
        hb = h.astype(BF16)
        f = None
        for k in range(D_FF // FF_CHUNK):
            a = jnp.maximum(_dot(hb, w1_ref[:, k * FF_CHUNK:(k + 1) * FF_CHUNK]), 0.0)
            part = _dot((a * a).astype(BF16), w2_ref[k * FF_CHUNK:(k + 1) * FF_CHUNK, :])
            f = part if f is None else f + part
        o = x + rows(g2 * ng_ref[3:4, :]) * _rms(f)
        if n_blk == 1:
            o_ref[...] = o.reshape(nbt, lt, D_MODEL)
        else:
            o_ref[0, r, :] = o


def _ffn_call(x, mod, ng, w1, w2, *, nbt, lt):
    nb, seq, _ = x.shape

    def resident(shape):
        return pl.BlockSpec(shape, lambda bi, ti: (0, 0), pipeline_mode=pl.Buffered(1))

    return pl.pallas_call(
        functools.partial(_ffn_kernel, nbt=nbt, lt=lt),
        grid=(nb // nbt, seq // lt),
        in_specs=[
            pl.BlockSpec((nbt, lt, D_MODEL), lambda bi, ti: (bi, ti, 0)),
            pl.BlockSpec((nbt, N_MOD, D_MODEL), lambda bi, ti: (bi, 0, 0)),
            pl.BlockSpec((4, D_MODEL), lambda bi, ti: (0, 0)),
            resident((D_MODEL, D_FF)),
            resident((D_FF, D_MODEL)),
        ],
        out_specs=pl.BlockSpec((nbt, lt, D_MODEL), lambda bi, ti: (bi, ti, 0)),
        out_shape=jax.ShapeDtypeStruct((nb, seq, D_MODEL), F32),
        compiler_params=pltpu.CompilerParams(
            dimension_semantics=("arbitrary", "arbitrary"), vmem_limit_bytes=VMEM_LIMIT),
        name="ffn",
    )(x, mod, ng, w1, w2)


def _tile_plan(nb, seq):
    if seq >= 512:
        return 1, 512, CMLP_CHUNK
    assert seq <= CMLP_CHUNK and seq % 16 == 0 and (nb * seq) % 8 == 0
    return nb, seq, seq


def _layer_params(l, w_in, w_out, w_ff1, w_ff2, conv_w, conv_b, dt_bias, a_log, d_skip,
                  ssd_norm_g, v_ln_g, v_ln_b):
    wi = w_in[l]
    z0, x0, d0, u0 = 0, SSD_WIDTH, SSD_WIDTH + CONV_DIM, SSD_WIDTH + CONV_DIM + SSD_HEADS
    reps = LANES // SSD_HEADS
    win = jnp.concatenate([
        wi[:, z0:x0], wi[:, x0:d0], wi[:, u0:u0 + CMLP_WIDTH], wi[:, u0 + CMLP_WIDTH:],
        jnp.tile(wi[:, d0:u0], (1, reps))], axis=1).astype(BF16)
    return dict(
        win=win, wout=w_out[l].astype(BF16), w1=w_ff1[l].astype(BF16), w2=w_ff2[l].astype(BF16),
        cw=conv_w[l], cb=conv_b[l].reshape(1, CONV_DIM),
        dtb=jnp.tile(dt_bias[l], reps).reshape(1, LANES),
        alog=jnp.tile(a_log[l], reps).reshape(1, LANES),
        dskip=jnp.repeat(d_skip[l], SSD_HEAD_DIM).reshape(1, SSD_WIDTH),
        sng=ssd_norm_g[l].reshape(1, SSD_WIDTH),
        vg=v_ln_g[l].reshape(1, CMLP_WIDTH), vb=v_ln_b[l].reshape(1, CMLP_WIDTH))


def _block(x, mod, cinit, sinit, p, ng, ws_l, bs_l, emit_v):
    nb, seq, _ = x.shape
    nbt, lt, ch = _tile_plan(nb, seq)
    ws = ws_l[:, :ch, :ch]
    bs = jnp.repeat(bs_l[:, :ch].T, CMLP_GDIM, axis=1)
    cin8 = jnp.pad(cinit, ((0, 0), (CONV_HDR - (CONV_W - 1), 0), (0, 0)))
    outs = _mixer_call(x, mod, cin8, sinit.reshape(nb, SSD_WIDTH, D_STATE), ng, p["win"], p["cw"], p["cb"],
                       p["dtb"], p["alog"], p["dskip"], p["sng"], p["vg"], p["vb"], ws, bs, p["wout"],
                       nbt=nbt, lt=lt, ch=ch, emit_v=emit_v)
    x1, cout, sout = outs[:3]
    ffn_lt = FFN_TILE if (nbt == 1 and seq % FFN_TILE == 0) else lt
    x2 = _ffn_call(x1, mod, ng, p["w1"], p["w2"], nbt=nbt, lt=ffn_lt)
    conv_state = cout[:, CONV_HDR - (CONV_W - 1):, :]
    ssd_state = sout.reshape(nb, SSD_HEADS, SSD_HEAD_DIM, D_STATE)
    return x2, conv_state, ssd_state, (outs[3] if emit_v else None)


def kernel(x_prompt, x_sample, state_conv, state_ssd, c_prompt, c_sample, w_mod, b_mod, norm_g, w_in, conv_w,
           conv_b, dt_bias, a_log, d_skip, ssd_norm_g, v_ln_g, v_ln_b, w_s, b_s, w_out, w_ff1, w_ff2):
    bp = x_prompt.shape[0]
    bs_ = x_sample.shape[0]
    mod = _mod_call(jnp.concatenate([c_prompt, c_sample], axis=0), w_mod, b_mod)
    mod = mod.reshape(DEPTH, bp + bs_, N_MOD, D_MODEL)
    conv_p0 = jnp.zeros((bp, CONV_W - 1, CONV_DIM), F32)
    ssd_p0 = jnp.zeros((bp, SSD_HEADS, SSD_HEAD_DIM, D_STATE), F32)
    xp, xs = x_prompt, x_sample
    conv_p, ssd_p, conv_s, ssd_s, v_s = [], [], [], [], []
    for l in range(DEPTH):
        p = _layer_params(l, w_in, w_out, w_ff1, w_ff2, conv_w, conv_b, dt_bias, a_log, d_skip,
                          ssd_norm_g, v_ln_g, v_ln_b)
        xp, cp, hp, _ = _block(xp, mod[l, :bp], conv_p0, ssd_p0, p, norm_g[l], w_s[l], b_s[l], False)
        xs, cs, hs, vs = _block(xs, mod[l, bp:], state_conv[l], state_ssd[l], p, norm_g[l], w_s[l], b_s[l], True)
        conv_p.append(cp)
        ssd_p.append(hp)
        conv_s.append(cs)
        ssd_s.append(hs)
        v_s.append(vs)
    return (xp, xs, jnp.stack(conv_p), jnp.stack(ssd_p), jnp.stack(conv_s), jnp.stack(ssd_s), jnp.stack(v_s))
```

```python
import functools

import numpy as np
import jax
import jax.numpy as jnp
from jax import lax
from jax.experimental import pallas as pl
from jax.experimental.pallas import tpu as pltpu

F32 = jnp.float32
BF16 = jnp.bfloat16

D_MODEL = 1024
DEPTH = 4
SSD_WIDTH = 512
SSD_HEAD_DIM = 64
SSD_HEADS = 8
SSD_GROUPS = 2
SSD_HPG = 4
D_STATE = 128
CONV_W = 4
CONV_DIM = 1024
CMLP_WIDTH = 512
CMLP_GROUPS = 4
CMLP_GDIM = 128
CMLP_CHUNK = 128
D_FF = 4096
N_MOD = 6
RMS_EPS = 1e-6
LN_EPS = 1e-5

LANES = 128
CONV_HDR = 8
N_SLABS = CONV_DIM // LANES
GROUP_W = SSD_HPG * SSD_HEAD_DIM
OFF_Z, OFF_XBC, OFF_U, OFF_V, OFF_DT = 0, 512, 1536, 2048, 2560
PROJ_W = OFF_DT + LANES
VMEM_LIMIT = 56 * 1024 * 1024


def _silu(x):
    h = 0.5 * x
    return h + h * jnp.tanh(h)


GELU_C0 = float(np.sqrt(2.0 / np.pi))
GELU_C1 = GELU_C0 * 0.044715


def _gelu_tanh(x):
    h = 0.5 * x
    return h + h * jnp.tanh(x * (GELU_C0 + GELU_C1 * (x * x)))


def _rms(x, eps=RMS_EPS):
    return x * lax.rsqrt(jnp.mean(x * x, axis=-1, keepdims=True) + eps)


def _dot(a, b):
    return jnp.dot(a, b, preferred_element_type=F32)


def _mod_kernel(c_ref, w_ref, b_ref, o_ref):
    s = _silu(c_ref[...]).astype(BF16)
    o_ref[0] = _dot(s, w_ref[0].astype(BF16)) + b_ref[0]


def _mod_call(c_all, w_mod, b_mod):
    nb = c_all.shape[0]
    tn = 1536
    n = N_MOD * D_MODEL
    return pl.pallas_call(
        _mod_kernel,
        grid=(DEPTH, n // tn),
        in_specs=[
            pl.BlockSpec((nb, D_MODEL), lambda l, j: (0, 0)),
            pl.BlockSpec((1, D_MODEL, tn), lambda l, j: (l, 0, j)),
            pl.BlockSpec((1, 1, tn), lambda l, j: (l, 0, j)),
        ],
        out_specs=pl.BlockSpec((1, nb, tn), lambda l, j: (l, 0, j)),
        out_shape=jax.ShapeDtypeStruct((DEPTH, nb, n), F32),
        compiler_params=pltpu.CompilerParams(
            dimension_semantics=("arbitrary", "arbitrary"), vmem_limit_bytes=VMEM_LIMIT),
        name="mod",
    )(c_all, w_mod, b_mod.reshape(DEPTH, 1, n))


N_SLOT_BUFS = 6


def _mixer_kernel(xa_ref, xc_ref, moda_ref, modc_ref, cinit_ref, sinit_ref, ng_ref, win_ref, cw_ref, cb_ref,
                  dtb_ref, alog_ref, dskip_ref, sng_ref, vg_ref, vb_ref, ws_ref, bs_ref,
                  wout_ref,
                  *refs, nbt, lt, ch, nt, n_tiles, pipelined, emit_v):
    n_out = 4 if emit_v else 3
    x1_ref, cout_ref, sout_ref = refs[:3]
    vout_ref = refs[3] if emit_v else None
    scratch = refs[n_out:]
    slots = [scratch[k * N_SLOT_BUFS:(k + 1) * N_SLOT_BUFS] for k in range(2 if pipelined else 1)]
    s_ref, *xa_refs = scratch[len(slots) * N_SLOT_BUFS:]
    stack_k = ch == LANES
    strided_conv = len(xa_refs) > 0

    s_id = pl.program_id(0)
    tm = nbt * lt
    nc = lt // ch

    def rows(m):
        if nbt == 1:
            return m
        return jnp.broadcast_to(m[:, None, :], (nbt, lt, m.shape[-1])).reshape(tm, m.shape[-1])

    n_sub = ch // 8
    sub_i = lax.broadcasted_iota(jnp.int32, (n_sub, 8, LANES), 1)
    lane_i = lax.broadcasted_iota(jnp.int32, (ch, LANES), 1)
    ii = lax.broadcasted_iota(jnp.int32, (ch, ch), 0)
    jj = lax.broadcasted_iota(jnp.int32, (ch, ch), 1)
    causal = ii >= jj
    head_of_lane = lax.broadcasted_iota(jnp.int32, (ch, GROUP_W), 1) // SSD_HEAD_DIM

    def expand_heads(v):
        cols = [jnp.broadcast_to(v[:, h:h + 1], (ch, LANES)) for h in range(SSD_HEADS)]
        return jnp.concatenate([jnp.where(lane_i < SSD_HEAD_DIM, cols[2 * q], cols[2 * q + 1])
                                for q in range(SSD_HEADS // 2)], axis=1)

    def transpose_rows(v):
        if ch == LANES:
            return v.T
        return jnp.concatenate([v, jnp.zeros((LANES - ch, LANES), F32)], axis=0).T[:, :ch]

    def cumsum_rows(a):
        a3 = a.reshape(n_sub, 8, LANES)
        for sft in (1, 2, 4):
            a3 = a3 + jnp.where(sub_i >= sft, pltpu.roll(a3, sft, 1), 0.0)
        carry = jnp.zeros((1, LANES), F32)
        out = []
        for k in range(n_sub):
            out.append(a3[k] + carry)
            carry = carry + a3[k, 7:8, :]
        return jnp.concatenate(out, axis=0)

    def project(slot):
        xbc_ref, z_ref, dt_ref, u_ref, v_ref, _ = slot
        x = xa_ref[...].reshape(tm, D_MODEL)
        mod = moda_ref[...]
        sh1, sc1 = mod[:, 0, :], mod[:, 1, :]
        h = _rms(x) * rows(ng_ref[0:1, :] * (1.0 + sc1)) + rows(sh1)
        hb = h.astype(BF16)
        yield
        z_ref[...] = _silu(_dot(hb, win_ref[:, OFF_Z:OFF_XBC]))
        yield
        xbc = _dot(hb, win_ref[:, OFF_XBC:OFF_U])
        for b in range(nbt):
            for j in range(N_SLABS):
                xbc_ref[j, b, CONV_HDR:CONV_HDR + lt, :] = xbc[b * lt:(b + 1) * lt, j * LANES:(j + 1) * LANES]
        yield
        u_ref[...] = _gelu_tanh(_dot(hb, win_ref[:, OFF_U:OFF_V]))
        yield
        vd = _dot(hb, win_ref[:, OFF_V:PROJ_W])
        v_ref[...] = _gelu_tanh(vd[:, :CMLP_WIDTH])
        dt_ref[...] = vd[:, CMLP_WIDTH:]
        yield

    def init_tile(slot):
        xbc_ref = slot[0]
        for b in range(nbt):
            for j in range(N_SLABS):
                xbc_ref[j, b, 0:CONV_HDR, :] = cinit_ref[b, :, j * LANES:(j + 1) * LANES]
        if nbt == 1:
            for g in range(SSD_GROUPS):
                s_ref[g] = sinit_ref[0, g * GROUP_W:(g + 1) * GROUP_W, :].T

    def finish_tile(slot):
        xbc_ref = slot[0]
        for b in range(nbt):
            for j in range(N_SLABS):
                cout_ref[b, :, j * LANES:(j + 1) * LANES] = xbc_ref[j, b, lt:lt + CONV_HDR, :]
        if nbt == 1:
            for g in range(SSD_GROUPS):
                sout_ref[0, g * GROUP_W:(g + 1) * GROUP_W, :] = s_ref[g].T

    def mix(slot, next_slot):
        xbc_ref, z_ref, dt_ref, u_ref, v_ref, y_ref = slot
        a_row = -jnp.exp(alog_ref[...])
        w_mix = [jnp.where(causal, ws_ref[g], 0.0).astype(BF16) for g in range(CMLP_GROUPS)]
        if stack_k:
            w_mix = [jnp.concatenate(w_mix[2 * q:2 * q + 2], axis=1) for q in range(CMLP_GROUPS // 2)]
        zeros_gd = jnp.zeros((ch, CMLP_GDIM), BF16)
        for i in range(nbt * nc):
            b, c = divmod(i, nc)
            r0 = c * ch
            rs = slice(i * ch, (i + 1) * ch)

            if strided_conv:
                xa_ref = xa_refs[i]
                first = CONV_HDR - (CONV_W - 1)
                for j in range(N_SLABS):
                    col = slice(j * LANES, (j + 1) * LANES)
                    phase = {m: xbc_ref[j, b, pl.ds(r0 + m, n_sub, stride=8), :]
                             for m in range(first, first + 8 + CONV_W - 1)}
                    for p in range(8):
                        acc = cb_ref[:, col]
                        for k in range(CONV_W):
                            acc = acc + cw_ref[k:k + 1, col] * phase[first + p + k]
                        xa_ref[j, pl.ds(p, n_sub, stride=8), :] = _silu(acc)
                xbc_a = jnp.concatenate([xa_ref[j] for j in range(N_SLABS)], axis=1)
            else:
                big = jnp.concatenate([xbc_ref[j, b, r0:r0 + ch + CONV_HDR, :] for j in range(N_SLABS)], axis=1)
                acc = cb_ref[...] + cw_ref[CONV_W - 1:CONV_W, :] * big[CONV_HDR:, :]
                for k in range(CONV_W - 1):
                    acc = acc + cw_ref[k:k + 1, :] * pltpu.roll(big, CONV_W - 1 - k, 0)[CONV_HDR:, :]
                xbc_a = _silu(acc)
            xs = xbc_a[:, :SSD_WIDTH]
            xsb = xs.astype(BF16)
            bmb = xbc_a[:, SSD_WIDTH:SSD_WIDTH + SSD_GROUPS * D_STATE].astype(BF16)
            cmb = xbc_a[:, SSD_WIDTH + SSD_GROUPS * D_STATE:].astype(BF16)

            dtr = dt_ref[rs, :] + dtb_ref[...]
            dt = jnp.maximum(dtr, 0.0) + jnp.log1p(jnp.exp(-jnp.abs(dtr)))
            acum = cumsum_rows(dt * a_row)
            acum_e = expand_heads(acum)
            ea_e = jnp.exp(acum_e)
            xw = (xs * (jnp.exp(acum_e[ch - 1:ch, :] - acum_e) * expand_heads(dt))).astype(BF16)
            acum_t = transpose_rows(acum)
            dt_t = transpose_rows(dt)

            if nbt > 1:
                for g in range(SSD_GROUPS):
                    s_ref[g] = sinit_ref[b, g * GROUP_W:(g + 1) * GROUP_W, :].T

            ys = []
            for g in range(SSD_GROUPS):
                gs = slice(g * GROUP_W, (g + 1) * GROUP_W)
                cg = cmb[:, g * D_STATE:(g + 1) * D_STATE]
                bg = bmb[:, g * D_STATE:(g + 1) * D_STATE]
                cbm = lax.dot_general(cg, bg, (((1,), (1,)), ((), ())), preferred_element_type=F32)
                mms, rhss = [], []
                for r in range(SSD_HPG):
                    hh = g * SSD_HPG + r
                    dec = jnp.where(causal, jnp.exp(acum[:, hh:hh + 1] - acum_t[hh:hh + 1, :]), 0.0)
                    mms.append((cbm * (dec * dt_t[hh:hh + 1, :])).astype(BF16))
                    rhss.append(jnp.where(head_of_lane == r, xsb[:, gs], jnp.zeros((ch, GROUP_W), BF16)))
                if stack_k:
                    mms = [jnp.concatenate(mms[2 * q:2 * q + 2], axis=1) for q in range(SSD_HPG // 2)]
                    rhss = [jnp.concatenate(rhss[2 * q:2 * q + 2], axis=0) for q in range(SSD_HPG // 2)]
                yg = None
                for mm, rhs in zip(mms, rhss):
                    part = _dot(mm, rhs)
                    yg = part if yg is None else yg + part
                sg = s_ref[g]
                yg = yg + _dot(cg, sg.astype(BF16)) * ea_e[:, gs]
                ds = lax.dot_general(bg, xw[:, gs], (((0,), (0,)), ((), ())), preferred_element_type=F32)
                s_ref[g] = ea_e[ch - 1:ch, gs] * sg + ds
                ys.append(yg)
            y = jnp.concatenate(ys, axis=1) + dskip_ref[...] * xs
            y = _rms(y * z_ref[rs, :]) * sng_ref[...]
            y_ref[rs, 0:SSD_WIDTH] = y.astype(BF16)

            if nbt > 1:
                for g in range(SSD_GROUPS):
                    sout_ref[b, g * GROUP_W:(g + 1) * GROUP_W, :] = s_ref[g].T

            u = u_ref[rs, :]
            v = v_ref[rs, :]
            vc = v - jnp.mean(v, axis=-1, keepdims=True)
            vn = vc * lax.rsqrt(jnp.mean(vc * vc, axis=-1, keepdims=True) + LN_EPS) * vg_ref[...] + vb_ref[...]
            if emit_v:
                vout_ref[b, r0:r0 + ch, :] = vn
            vnb = vn.astype(BF16)
            gw = 2 * CMLP_GDIM if stack_k else CMLP_GDIM
            for q, wq in enumerate(w_mix):
                sl = slice(q * gw, (q + 1) * gw)
                if stack_k:
                    va, vb_ = vnb[:, q * gw:q * gw + CMLP_GDIM], vnb[:, q * gw + CMLP_GDIM:(q + 1) * gw]
                    rhs = jnp.concatenate([jnp.concatenate([va, zeros_gd], axis=1),
                                           jnp.concatenate([zeros_gd, vb_], axis=1)], axis=0)
                else:
                    rhs = vnb[:, sl]
                mixed = _dot(wq, rhs) + bs_ref[:, sl]
                y_ref[rs, SSD_WIDTH + q * gw:SSD_WIDTH + (q + 1) * gw] = (u[:, sl] * mixed).astype(BF16)
            yield

        for b in range(nbt):
            for j in range(N_SLABS):
                next_slot[0][j, b, 0:CONV_HDR, :] = xbc_ref[j, b, lt:lt + CONV_HDR, :]

    def emit(slot):
        y_ref = slot[5]
        x = xc_ref[...].reshape(tm, D_MODEL)
        g1 = modc_ref[...][:, 2, :]
        m = _dot(y_ref[...], wout_ref[...])
        yield
        x1 = x + rows(g1 * ng_ref[1:2, :]) * _rms(m)
        x1_ref[...] = x1.reshape(nbt, lt, D_MODEL)

    def run(gens, order):
        for k in order:
            next(gens[k], None)
        for g in gens:
            for _ in g:
                pass

    if not pipelined:
        assert nt == 1
        init_tile(slots[0])
        run([project(slots[0]), mix(slots[0], slots[0]), emit(slots[0])], ())
        finish_tile(slots[0])
        return

    assert nbt == 1
    tb = jnp.clip(s_id - 1, 0, n_tiles - 1) % nt
    mix_is_real = jnp.logical_and(s_id >= 1, s_id <= n_tiles)

    @pl.when(s_id == 0)
    def _():
        for ref in slots[1]:
            ref[...] = jnp.zeros(ref.shape, ref.dtype)
        slots[0][5][...] = jnp.zeros(slots[0][5].shape, BF16)

    for par in range(2):
        @pl.when(s_id % 2 == par)
        def _(par=par):
            mine, other = slots[1 - par], slots[par]
            pl.when(tb == 0)(lambda: init_tile(mine))
            run([project(other), mix(mine, other), emit(other)], (2, 0, 1, 0, 0, 2, 1, 0, 1, 0, 1, 0))
            pl.when(jnp.logical_and(tb == nt - 1, mix_is_real))(lambda: finish_tile(mine))


def _mixer_call(x, mod, cinit, sinit, ng, win, cw, cb, dtb, alog, dskip, sng, vg, vb, ws, bs,
                wout, *, nbt, lt, ch, emit_v):
    nb, seq, _ = x.shape
    tm = nbt * lt
    nt = seq // lt
    n_tiles = (nb // nbt) * nt
    pipelined = n_tiles > 1
    lag = 1 if pipelined else 0
    n_steps = n_tiles + 2 * lag

    def tile_a(s):
        return jnp.clip(s, 0, n_tiles - 1)

    def tile_b(s):
        return jnp.clip(s - lag, 0, n_tiles - 1)

    def tile_c(s):
        return jnp.clip(s - 2 * lag, 0, n_tiles - 1)

    def const(shape):
        return pl.BlockSpec(shape, lambda s: (0,) * len(shape))

    def per_seq(shape, tile):
        return pl.BlockSpec(shape, lambda s: (tile(s) // nt, 0, 0))

    def per_tile(shape, tile):
        return pl.BlockSpec(shape, lambda s: (tile(s) // nt, tile(s) % nt, 0))

    in_specs = [
        per_tile((nbt, lt, D_MODEL), tile_a),
        per_tile((nbt, lt, D_MODEL), tile_c),
        per_seq((nbt, N_MOD, D_MODEL), tile_a),
        per_seq((nbt, N_MOD, D_MODEL), tile_c),
        per_seq((nbt, CONV_HDR, CONV_DIM), tile_b),
        per_seq((nbt, SSD_WIDTH, D_STATE), tile_b),
        const((4, D_MODEL)),
        const((D_MODEL, PROJ_W)),
        const((CONV_W, CONV_DIM)),
        const((1, CONV_DIM)),
        const((1, LANES)),
        const((1, LANES)),
        const((1, SSD_WIDTH)),
        const((1, SSD_WIDTH)),
        const((1, CMLP_WIDTH)),
        const((1, CMLP_WIDTH)),
        const((CMLP_GROUPS, ch, ch)),
        const((ch, CMLP_WIDTH)),
        const((D_MODEL, D_MODEL)),
    ]
    out_specs = [
        per_tile((nbt, lt, D_MODEL), tile_c),
        per_seq((nbt, CONV_HDR, CONV_DIM), tile_b),
        per_seq((nbt, SSD_WIDTH, D_STATE), tile_b),
    ]
    out_shape = [
        jax.ShapeDtypeStruct((nb, seq, D_MODEL), F32),
        jax.ShapeDtypeStruct((nb, CONV_HDR, CONV_DIM), F32),
        jax.ShapeDtypeStruct((nb, SSD_WIDTH, D_STATE), F32),
    ]
    if emit_v:
        out_specs.append(per_tile((nbt, lt, CMLP_WIDTH), tile_b))
        out_shape.append(jax.ShapeDtypeStruct((nb, seq, CMLP_WIDTH), F32))
    slot = [
        pltpu.VMEM((N_SLABS, nbt, lt + CONV_HDR, LANES), F32),
        pltpu.VMEM((tm, SSD_WIDTH), F32),
        pltpu.VMEM((tm, LANES), F32),
        pltpu.VMEM((tm, CMLP_WIDTH), F32),
        pltpu.VMEM((tm, CMLP_WIDTH), F32),
        pltpu.VMEM((tm, D_MODEL), BF16),
    ]
    assert len(slot) == N_SLOT_BUFS
    scratch = slot * (2 if pipelined else 1) + [
        pltpu.VMEM((SSD_GROUPS, D_STATE, GROUP_W), F32),
    ]
    if ch == LANES:
        scratch += [pltpu.VMEM((N_SLABS, ch, LANES), F32)] * (tm // ch)
    return pl.pallas_call(
        functools.partial(_mixer_kernel, nbt=nbt, lt=lt, ch=ch, nt=nt, n_tiles=n_tiles, pipelined=pipelined,
                          emit_v=emit_v),
        grid=(n_steps,),
        in_specs=in_specs,
        out_specs=out_specs,
        out_shape=out_shape,
        scratch_shapes=scratch,
        compiler_params=pltpu.CompilerParams(
            dimension_semantics=("arbitrary",), vmem_limit_bytes=VMEM_LIMIT),
        name="mixer",
    )(x, x, mod, mod, cinit, sinit, ng, win, cw, cb, dtb, alog, dskip, sng, vg, vb, ws, bs, wout)


FF_CHUNK = 1024


FFN_ROWS = 256
FFN_TILE = 2048


def _ffn_kernel(x_ref, mod_ref, ng_ref, w1_ref, w2_ref, o_ref, *, nbt, lt):
    tm = nbt * lt

    def rows(m):
        if nbt == 1:
            return m
        return jnp.broadcast_to(m[:, None, :], (nbt, lt, m.shape[-1])).reshape(tm, m.shape[-1])

    mod = mod_ref[...]
    sh2, sc2, g2 = mod[:, 3, :], mod[:, 4, :], mod[:, 5, :]
    n_blk = lt // FFN_ROWS if (nbt == 1 and lt % FFN_ROWS == 0) else 1
    for k_blk in range(n_blk):
        if n_blk == 1:
            x = x_ref[...].reshape(tm, D_MODEL)
        else:
            r = slice(k_blk * FFN_ROWS, (k_blk + 1) * FFN_ROWS)
            x = x_ref[0, r, :]
        h = _rms(x) * rows(ng_ref[2:3, :] * (1.0 + sc2)) + rows(sh2)
        hb = h.astype(BF16)
        f = None
        for k in range(D_FF // FF_CHUNK):
            a = jnp.maximum(_dot(hb, w1_ref[:, k * FF_CHUNK:(k + 1) * FF_CHUNK]), 0.0)
            part = _dot((a * a).astype(BF16), w2_ref[k * FF_CHUNK:(k + 1) * FF_CHUNK, :])
            f = part if f is None else f + part
        o = x + rows(g2 * ng_ref[3:4, :]) * _rms(f)
        if n_blk == 1:
            o_ref[...] = o.reshape(nbt, lt, D_MODEL)
        else:
            o_ref[0, r, :] = o


def _ffn_call(x, mod, ng, w1, w2, *, nbt, lt):
    nb, seq, _ = x.shape

    def resident(shape):
        return pl.BlockSpec(shape, lambda bi, ti: (0, 0), pipeline_mode=pl.Buffered(1))

    return pl.pallas_call(
        functools.partial(_ffn_kernel, nbt=nbt, lt=lt),
        grid=(nb // nbt, seq // lt),
        in_specs=[
            pl.BlockSpec((nbt, lt, D_MODEL), lambda bi, ti: (bi, ti, 0)),
            pl.BlockSpec((nbt, N_MOD, D_MODEL), lambda bi, ti: (bi, 0, 0)),
            pl.BlockSpec((4, D_MODEL), lambda bi, ti: (0, 0)),
            resident((D_MODEL, D_FF)),
            resident((D_FF, D_MODEL)),
        ],
        out_specs=pl.BlockSpec((nbt, lt, D_MODEL), lambda bi, ti: (bi, ti, 0)),
        out_shape=jax.ShapeDtypeStruct((nb, seq, D_MODEL), F32),
        compiler_params=pltpu.CompilerParams(
            dimension_semantics=("arbitrary", "arbitrary"), vmem_limit_bytes=VMEM_LIMIT),
        name="ffn",
    )(x, mod, ng, w1, w2)


def _tile_plan(nb, seq):
    if seq >= 512:
        return 1, 512, CMLP_CHUNK
    assert seq <= CMLP_CHUNK and seq % 16 == 0 and (nb * seq) % 8 == 0
    return nb, seq, seq


def _layer_params(l, w_in, w_out, w_ff1, w_ff2, conv_w, conv_b, dt_bias, a_log, d_skip,
                  ssd_norm_g, v_ln_g, v_ln_b):
    wi = w_in[l]
    z0, x0, d0, u0 = 0, SSD_WIDTH, SSD_WIDTH + CONV_DIM, SSD_WIDTH + CONV_DIM + SSD_HEADS
    reps = LANES // SSD_HEADS
    win = jnp.concatenate([
        wi[:, z0:x0], wi[:, x0:d0], wi[:, u0:u0 + CMLP_WIDTH], wi[:, u0 + CMLP_WIDTH:],
        jnp.tile(wi[:, d0:u0], (1, reps))], axis=1).astype(BF16)
    return dict(
        win=win, wout=w_out[l].astype(BF16), w1=w_ff1[l].astype(BF16), w2=w_ff2[l].astype(BF16),
        cw=conv_w[l], cb=conv_b[l].reshape(1, CONV_DIM),
        dtb=jnp.tile(dt_bias[l], reps).reshape(1, LANES),
        alog=jnp.tile(a_log[l], reps).reshape(1, LANES),
        dskip=jnp.repeat(d_skip[l], SSD_HEAD_DIM).reshape(1, SSD_WIDTH),
        sng=ssd_norm_g[l].reshape(1, SSD_WIDTH),
        vg=v_ln_g[l].reshape(1, CMLP_WIDTH), vb=v_ln_b[l].reshape(1, CMLP_WIDTH))


def _block(x, mod, cinit, sinit, p, ng, ws_l, bs_l, emit_v):
    nb, seq, _ = x.shape
    nbt, lt, ch = _tile_plan(nb, seq)
    ws = ws_l[:, :ch, :ch]
    bs = jnp.repeat(bs_l[:, :ch].T, CMLP_GDIM, axis=1)
    cin8 = jnp.pad(cinit, ((0, 0), (CONV_HDR - (CONV_W - 1), 0), (0, 0)))
    outs = _mixer_call(x, mod, cin8, sinit.reshape(nb, SSD_WIDTH, D_STATE), ng, p["win"], p["cw"], p["cb"],
                       p["dtb"], p["alog"], p["dskip"], p["sng"], p["vg"], p["vb"], ws, bs, p["wout"],
                       nbt=nbt, lt=lt, ch=ch, emit_v=emit_v)
    x1, cout, sout = outs[:3]
    ffn_lt = FFN_TILE if (nbt == 1 and seq % FFN_TILE == 0) else lt
    x2 = _ffn_call(x1, mod, ng, p["w1"], p["w2"], nbt=nbt, lt=ffn_lt)
    conv_state = cout[:, CONV_HDR - (CONV_W - 1):, :]
    ssd_state = sout.reshape(nb, SSD_HEADS, SSD_HEAD_DIM, D_STATE)
    return x2, conv_state, ssd_state, (outs[3] if emit_v else None)


def kernel(x_prompt, x_sample, state_conv, state_ssd, c_prompt, c_sample, w_mod, b_mod, norm_g, w_in, conv_w,
           conv_b, dt_bias, a_log, d_skip, ssd_norm_g, v_ln_g, v_ln_b, w_s, b_s, w_out, w_ff1, w_ff2):
    bp = x_prompt.shape[0]
    bs_ = x_sample.shape[0]
    mod = _mod_call(jnp.concatenate([c_prompt, c_sample], axis=0), w_mod, b_mod)
    mod = mod.reshape(DEPTH, bp + bs_, N_MOD, D_MODEL)
    conv_p0 = jnp.zeros((bp, CONV_W - 1, CONV_DIM), F32)
    ssd_p0 = jnp.zeros((bp, SSD_HEADS, SSD_HEAD_DIM, D_STATE), F32)
    xp, xs = x_prompt, x_sample
    conv_p, ssd_p, conv_s, ssd_s, v_s = [], [], [], [], []
    for l in range(DEPTH):
        p = _layer_params(l, w_in, w_out, w_ff1, w_ff2, conv_w, conv_b, dt_bias, a_log, d_skip,
                          ssd_norm_g, v_ln_g, v_ln_b)
        xp, cp, hp, _ = _block(xp, mod[l, :bp], conv_p0, ssd_p0, p, norm_g[l], w_s[l], b_s[l], False)
        xs, cs, hs, vs = _block(xs, mod[l, bp:], state_conv[l], state_ssd[l], p, norm_g[l], w_s[l], b_s[l], True)
        conv_p.append(cp)
        ssd_p.append(hp)
        conv_s.append(cs)
        ssd_s.append(hs)
        v_s.append(vs)
    return (xp, xs, jnp.stack(conv_p), jnp.stack(ssd_p), jnp.stack(conv_s), jnp.stack(ssd_s), jnp.stack(v_s))
```
